```python
import math
import jax, jax.numpy as jnp
from jax import lax
import numpy as np

D_MODEL = 2048
BATCH = 4
SEQ = 2048
DEPTH = 4
DEC_BATCH = 128
DEC_SEQ = 8
PAST_LEN = 16384
PAGE_SIZE = 128

N_MIXERS = 4
N_SSD = (DEPTH + 3) // 4
N_RWKV = (DEPTH + 2) // 4
N_LRU = (DEPTH + 1) // 4
N_GDN = DEPTH // 4

CONV_W = 4
NORM_EPS = 1e-6
L2_EPS = 1e-6

SSD_D_INNER = 2 * D_MODEL
SSD_HEADDIM = 64
SSD_HEADS = SSD_D_INNER // SSD_HEADDIM
SSD_GROUPS = 8
SSD_HPG = SSD_HEADS // SSD_GROUPS
SSD_STATE = 128
SSD_CONV_DIM = SSD_D_INNER + 2 * SSD_GROUPS * SSD_STATE
SSD_IN_DIM = SSD_D_INNER + SSD_CONV_DIM + SSD_HEADS
SSD_CHUNK = 64

RWKV_HEAD = 64
RWKV_HEADS = D_MODEL // RWKV_HEAD
RWKV_DECAY_LORA = 96
RWKV_AAA_LORA = 96
RWKV_GATE_LORA = 256
RWKV_GN_EPS = 64e-5

LRU_WIDTH = D_MODEL
LRU_BLOCKS = 8
LRU_BLOCK = LRU_WIDTH // LRU_BLOCKS
LRU_C = 8.0

GDN_K_HEADS = 16
GDN_V_HEADS = 32
GDN_HEAD_K = 128
GDN_HEAD_V = 128
GDN_KEY_DIM = GDN_K_HEADS * GDN_HEAD_K
GDN_VAL_DIM = GDN_V_HEADS * GDN_HEAD_V
GDN_CONV_DIM = 2 * GDN_KEY_DIM + GDN_VAL_DIM
GDN_IN_DIM = GDN_CONV_DIM + GDN_VAL_DIM + 2 * GDN_V_HEADS
GDN_CHUNK = 64

N_EXPERTS = 32
TOP_K = 4
D_FF = 2048
SWIGLU_LIMIT = 7.0
SWIGLU_ALPHA = 1.702
MOE_BLOCK = 128

STATE_KEYS = ('state_ssd', 'cache_ssd_conv', 'state_rwkv', 'cache_rwkv_shift',
              'state_lru', 'cache_lru_conv', 'state_gdn', 'cache_gdn_conv')

kernel_name = 'hybrid_ssd_rwkv7_rglru_gdn_moe_step'


def rmsnorm(x, g):
    xf = x.astype(jnp.float32)
    y = xf * lax.rsqrt(jnp.mean(xf * xf, axis=-1, keepdims=True) + NORM_EPS)
    return (y * g).astype(x.dtype)


def _l2norm(x):
    xf = x.astype(jnp.float32)
    return xf * lax.rsqrt(jnp.sum(xf * xf, axis=-1, keepdims=True) + L2_EPS)


def _seq_first(t):
    return jnp.moveaxis(t, 1, 0)


def causal_conv(x, buf, w, b=None):
    seqlen = x.shape[1]
    xp = jnp.concatenate([buf.astype(x.dtype), x], axis=1)
    y = sum(xp[:, k:k + seqlen] * w[k] for k in range(CONV_W))
    if b is not None:
        y = y + b
    return y, xp[:, seqlen:]


def _linear_combine(e1, e2):
    a1, b1 = e1
    a2, b2 = e2
    return a1 * a2, a2 * b1 + b2


def ssd_mixer(h, conv_buf, s0, w_in, conv_w, conv_b, dt_bias, a_log, d_skip, norm_g, w_out):
    bsz, seqlen, _ = h.shape
    f32 = jnp.float32
    z, xbc, dt = jnp.split(h @ w_in, [SSD_D_INNER, SSD_D_INNER + SSD_CONV_DIM], axis=-1)
    xbc, new_buf = causal_conv(xbc, conv_buf, conv_w, conv_b)
    xbc = jax.nn.silu(xbc.astype(f32))
    xs, bm, cm = jnp.split(xbc, [SSD_D_INNER, SSD_D_INNER + SSD_GROUPS * SSD_STATE], axis=-1)
    q = math.gcd(seqlen, SSD_CHUNK)
    nc = seqlen // q
    xs = xs.reshape(bsz, nc, q, SSD_GROUPS, SSD_HPG, SSD_HEADDIM)
    bm = bm.reshape(bsz, nc, q, SSD_GROUPS, SSD_STATE)
    cm = cm.reshape(bsz, nc, q, SSD_GROUPS, SSD_STATE)
    dt = jax.nn.softplus(dt.astype(f32) + dt_bias).reshape(bsz, nc, q, SSD_GROUPS, SSD_HPG)
    a = -jnp.exp(a_log.astype(f32)).reshape(SSD_GROUPS, SSD_HPG)
    cum = jnp.cumsum(dt * a, axis=2)
    causal = jnp.tril(jnp.ones((q, q), dtype=bool))[:, :, None, None]
    seg = cum[:, :, :, None] - cum[:, :, None, :]
    decay = jnp.exp(jnp.where(causal, seg, -jnp.inf))
    cb = jnp.einsum('bctgn,bcsgn->bctsg', cm, bm)
    y = jnp.einsum('bctsg,bctsgj,bcsgj,bcsgjp->bctgjp', cb, decay, dt, xs)
    last = cum[:, :, -1]
    to_end = jnp.exp(last[:, :, None] - cum) * dt

    def chunk_step(s, inp):
        x_c, b_c, c_c, cum_c, end_c, last_c = inp
        y_c = jnp.einsum('btgn,bgjpn->btgjp', c_c, s) * jnp.exp(cum_c)[..., None]
        s = s * jnp.exp(last_c)[..., None, None] + jnp.einsum('bsgj,bsgn,bsgjp->bgjpn', end_c, b_c, x_c)
        return s, y_c

    s_init = s0.astype(f32).reshape(bsz, SSD_GROUPS, SSD_HPG, SSD_HEADDIM, SSD_STATE)
    s_fin, y_inter = lax.scan(chunk_step, s_init,
                              (_seq_first(xs), _seq_first(bm), _seq_first(cm),
                               _seq_first(cum), _seq_first(to_end), _seq_first(last)))
    y = y + jnp.moveaxis(y_inter, 0, 1) + d_skip.reshape(SSD_GROUPS, SSD_HPG)[:, :, None] * xs
    y = y.reshape(bsz, seqlen, SSD_D_INNER) * jax.nn.silu(z.astype(f32))
    yg = y.reshape(bsz, seqlen, SSD_GROUPS, SSD_D_INNER // SSD_GROUPS)
    yg = yg * lax.rsqrt(jnp.mean(yg * yg, axis=-1, keepdims=True) + NORM_EPS)
    y = yg.reshape(bsz, seqlen, SSD_D_INNER) * norm_g
    out = y.astype(h.dtype) @ w_out
    return out, new_buf, s_fin.reshape(bsz, SSD_HEADS, SSD_HEADDIM, SSD_STATE).astype(h.dtype)


def rwkv_mixer(h, shift_buf, s0, mu, w_rkv, w0, w1, w2, a0, a1, a2, g1, g2, k_k, k_a, r_k, lnx_g, lnx_b, w_o):
    bsz, seqlen, d = h.shape
    f32 = jnp.float32
    prev = jnp.concatenate([shift_buf[:, None, :].astype(h.dtype), h[:, :-1]], axis=1)
    xm = h[None] + (prev - h)[None] * mu[:, None, None, :]
    r, k, v = jnp.einsum('sbld,sde->sble', xm[:3], w_rkv).astype(f32)
    xw, xa, xg = xm[3], xm[4], xm[5]
    w_raw = (w0 + jnp.tanh(xw @ w1) @ w2).astype(f32)
    decay = jnp.exp(-jnp.exp(-jax.nn.softplus(-w_raw) - 0.5))
    a = jax.nn.sigmoid((a0 + (xa @ a1) @ a2).astype(f32))
    g = jax.nn.sigmoid(xg @ g1) @ g2

    def heads(t):
        return t.reshape(bsz, seqlen, RWKV_HEADS, RWKV_HEAD)

    kk = _l2norm(heads(k * k_k))
    k = k * (1.0 + (a - 1.0) * k_a)
    r_h, k_h, v_h, a_h, w_h = heads(r), heads(k), heads(v), heads(a), heads(decay)

    def tok_step(s, inp):
        r_t, w_t, k_t, v_t, kk_t, a_t = inp
        s = (s * w_t[:, :, None, :]
             - jnp.einsum('bhvk,bhk->bhv', s, kk_t)[..., None] * (kk_t * a_t)[:, :, None, :]
             + v_t[..., None] * k_t[:, :, None, :])
        return s, jnp.einsum('bhvk,bhk->bhv', s, r_t)

    s_fin, o = lax.scan(tok_step, s0.astype(f32),
                        (_seq_first(r_h), _seq_first(w_h), _seq_first(k_h),
                         _seq_first(v_h), _seq_first(kk), _seq_first(a_h)))
    o = jnp.moveaxis(o, 0, 1)
    mean = jnp.mean(o, axis=-1, keepdims=True)
    var = jnp.mean(jnp.square(o - mean), axis=-1, keepdims=True)
    o = ((o - mean) * lax.rsqrt(var + RWKV_GN_EPS)).reshape(bsz, seqlen, d) * lnx_g + lnx_b
    bonus = jnp.sum(r_h * k_h * r_k, axis=-1, keepdims=True) * v_h
    o = o + bonus.reshape(bsz, seqlen, d)
    out = (o * g).astype(h.dtype) @ w_o
    return out, h[:, -1], s_fin.astype(h.dtype)


def lru_mixer(h, conv_buf, h0, w_in, conv_w, conv_b, w_rg, b_rg, w_ig, b_ig, lam, w_out):
    bsz, seqlen, _ = h.shape
    f32 = jnp.float32
    xb, gb = jnp.split(h @ w_in, 2, axis=-1)
    xb, new_buf = causal_conv(xb, conv_buf, conv_w, conv_b)
    xb = xb.astype(f32)
    xh = xb.reshape(bsz, seqlen, LRU_BLOCKS, LRU_BLOCK)
    gate_r = jax.nn.sigmoid(jnp.einsum('blhi,hij->blhj', xh, w_rg) + b_rg).reshape(bsz, seqlen, LRU_WIDTH)
    gate_i = jax.nn.sigmoid(jnp.einsum('blhi,hij->blhj', xh, w_ig) + b_ig).reshape(bsz, seqlen, LRU_WIDTH)
    log_a = -LRU_C * gate_r * jax.nn.softplus(-lam.astype(f32))
    a = jnp.exp(log_a)
    bx = jnp.sqrt(-jnp.expm1(2.0 * log_a)) * gate_i * xb
    bx = bx.at[:, 0].add(a[:, 0] * h0.astype(f32))
    _, hs = lax.associative_scan(_linear_combine, (a, bx), axis=1)
    y = hs * jax.nn.gelu(gb.astype(f32))
    return y.astype(h.dtype) @ w_out, new_buf, hs[:, -1].astype(h.dtype)


def gdn_mixer(h, conv_buf, s0, w_in, conv_w, a_log, dt_bias, norm_g, w_out):
    bsz, seqlen, _ = h.shape
    f32 = jnp.float32
    qkv, z, b_raw, a_raw = jnp.split(
        h @ w_in, [GDN_CONV_DIM, GDN_CONV_DIM + GDN_VAL_DIM, GDN_CONV_DIM + GDN_VAL_DIM + GDN_V_HEADS], axis=-1)
    qkv, new_buf = causal_conv(qkv, conv_buf, conv_w)
    qkv = jax.nn.silu(qkv.astype(f32))
    q, k, v = jnp.split(qkv, [GDN_KEY_DIM, 2 * GDN_KEY_DIM], axis=-1)
    rep = GDN_V_HEADS // GDN_K_HEADS
    q = jnp.repeat(_l2norm(q.reshape(bsz, seqlen, GDN_K_HEADS, GDN_HEAD_K)), rep, axis=2) * GDN_HEAD_K ** -0.5
    k = jnp.repeat(_l2norm(k.reshape(bsz, seqlen, GDN_K_HEADS, GDN_HEAD_K)), rep, axis=2)
    v = v.reshape(bsz, seqlen, GDN_V_HEADS, GDN_HEAD_V)
    beta = jax.nn.sigmoid(b_raw.astype(f32))
    g = -jnp.exp(a_log.astype(f32)) * jax.nn.softplus(a_raw.astype(f32) + dt_bias)
    cq = math.gcd(seqlen, GDN_CHUNK)
    nc = seqlen // cq

    def blk(t):
        return jnp.moveaxis(t.reshape(bsz, nc, cq, GDN_V_HEADS, -1), 3, 2)

    q, k, v, beta = blk(q), blk(k), blk(v), blk(beta)
    gc = jnp.cumsum(blk(g), axis=3)
    seg = gc - jnp.swapaxes(gc, -1, -2)
    tri_incl = jnp.tril(jnp.ones((cq, cq), dtype=bool))
    tri_strict = jnp.tril(jnp.ones((cq, cq), dtype=bool), -1)
    dec_incl = jnp.exp(jnp.where(tri_incl, seg, -jnp.inf))
    kb = k * beta
    a_mat = jnp.einsum('bchtd,bchsd->bchts', kb, k) * jnp.where(tri_strict, dec_incl, 0.0)
    rhs = jnp.concatenate([v * beta, kb * jnp.exp(gc)], axis=-1)
    sol = lax.linalg.triangular_solve(a_mat + jnp.eye(cq, dtype=f32), rhs,
                                      left_side=True, lower=True, unit_diagonal=True)
    u, w = jnp.split(sol, [GDN_HEAD_V], axis=-1)
    qk = jnp.einsum('bchtd,bchsd->bchts', q, k) * dec_incl
    g_last = gc[:, :, :, -1:]
    q_dec = q * jnp.exp(gc)
    k_dec = k * jnp.exp(g_last - gc)

    def chunk_step(s, inp):
        u_c, w_c, qk_c, qd_c, kd_c, gl_c = inp
        v_new = u_c - jnp.einsum('bhtk,bhkv->bhtv', w_c, s)
        o_c = jnp.einsum('bhtk,bhkv->bhtv', qd_c, s) + jnp.einsum('bhts,bhsv->bhtv', qk_c, v_new)
        s = s * jnp.exp(gl_c) + jnp.einsum('bhsk,bhsv->bhkv', kd_c, v_new)
        return s, o_c

    s_fin, o = lax.scan(chunk_step, s0.astype(f32),
                        (_seq_first(u), _seq_first(w), _seq_first(qk),
                         _seq_first(q_dec), _seq_first(k_dec), _seq_first(g_last)))
    o = jnp.swapaxes(jnp.moveaxis(o, 0, 1), 2, 3).reshape(bsz, seqlen, GDN_V_HEADS, GDN_HEAD_V)
    o = o * lax.rsqrt(jnp.mean(o * o, axis=-1, keepdims=True) + NORM_EPS) * norm_g
    o = o * jax.nn.silu(z.astype(f32).reshape(bsz, seqlen, GDN_V_HEADS, GDN_HEAD_V))
    out = o.reshape(bsz, seqlen, GDN_VAL_DIM).astype(h.dtype) @ w_out
    return out, new_buf, s_fin.astype(h.dtype)


def moe_ffn(h, w_router, b_router, w_gu, b_gu, w_down, b_down):
    bsz, seqlen, d = h.shape
    f32 = jnp.float32
    x = h.reshape(-1, d)
    n_tok = x.shape[0]
    logits = (x @ w_router + b_router).astype(f32)
    top_logit, top_e = lax.top_k(logits, TOP_K)
    gates = jax.nn.softmax(top_logit, axis=-1)
    n_pairs = n_tok * TOP_K
    flat_e = top_e.reshape(-1)
    flat_tok = jnp.arange(n_pairs, dtype=jnp.int32) // TOP_K
    order = jnp.argsort(flat_e)
    e_sorted = flat_e[order]
    counts = jnp.bincount(flat_e, length=N_EXPERTS)
    start = jnp.cumsum(counts) - counts
    padded = (counts + MOE_BLOCK - 1) // MOE_BLOCK * MOE_BLOCK
    pad_end = jnp.cumsum(padded)
    pad_start = pad_end - padded
    slot = pad_start[e_sorted] + jnp.arange(n_pairs, dtype=jnp.int32) - start[e_sorted]
    n_blocks = -(-n_pairs // MOE_BLOCK) + N_EXPERTS
    n_slots = n_blocks * MOE_BLOCK
    slot_tok = jnp.zeros((n_slots,), jnp.int32).at[slot].set(flat_tok[order])
    slot_gate = jnp.zeros((n_slots,), f32).at[slot].set(gates.reshape(-1)[order])
    block_e = jnp.minimum(jnp.searchsorted(pad_end, jnp.arange(n_blocks) * MOE_BLOCK, side='right'),
                          N_EXPERTS - 1)
    xb = x[slot_tok].reshape(n_blocks, MOE_BLOCK, d)

    def expert_block(args):
        xe, e = args
        gate, up = jnp.split(xe @ w_gu[e] + b_gu[e], 2, axis=-1)
        gate = jnp.minimum(gate, SWIGLU_LIMIT)
        up = jnp.clip(up, -SWIGLU_LIMIT, SWIGLU_LIMIT)
        glu = gate * jax.nn.sigmoid(SWIGLU_ALPHA * gate)
        return ((up + 1.0) * glu) @ w_down[e] + b_down[e]

    yb = lax.map(expert_block, (xb, block_e))
    y = jax.ops.segment_sum(yb.reshape(n_slots, d) * slot_gate[:, None], slot_tok, num_segments=n_tok)
    return y.reshape(bsz, seqlen, d).astype(h.dtype)


def _modulated_norm(x, g, shift, scale):
    return rmsnorm(x, g) * (1.0 + scale) + shift


def _zero_states(bsz, dtype):
    return {
        'state_ssd': jnp.zeros((N_SSD, bsz, SSD_HEADS, SSD_HEADDIM, SSD_STATE), dtype),
        'cache_ssd_conv': jnp.zeros((N_SSD, bsz, CONV_W - 1, SSD_CONV_DIM), dtype),
        'state_rwkv': jnp.zeros((N_RWKV, bsz, RWKV_HEADS, RWKV_HEAD, RWKV_HEAD), dtype),
        'cache_rwkv_shift': jnp.zeros((N_RWKV, bsz, D_MODEL), dtype),
        'state_lru': jnp.zeros((N_LRU, bsz, LRU_WIDTH), dtype),
        'cache_lru_conv': jnp.zeros((N_LRU, bsz, CONV_W - 1, LRU_WIDTH), dtype),
        'state_gdn': jnp.zeros((N_GDN, bsz, GDN_V_HEADS, GDN_HEAD_K, GDN_HEAD_V), dtype),
        'cache_gdn_conv': jnp.zeros((N_GDN, bsz, CONV_W - 1, GDN_CONV_DIM), dtype),
    }


def run_trunk(x, c, st, p):
    new = {key: [] for key in STATE_KEYS}
    cond = jax.nn.silu(c.astype(jnp.float32)).astype(x.dtype)
    for i in range(DEPTH):
        mixer, j = i % N_MIXERS, i // N_MIXERS
        mod = (cond @ p['w_ada'][i] + p['b_ada'][i])[:, None, :]
        sh1, sc1, gt1, sh2, sc2, gt2 = jnp.split(mod, 6, axis=-1)
        hn = _modulated_norm(x, p['norm_g'][i, 0], sh1, sc1)
        if mixer == 0:
            y, buf, s = ssd_mixer(hn, st['cache_ssd_conv'][j], st['state_ssd'][j],
                                  p['ssd_w_in'][j], p['ssd_conv_w'][j], p['ssd_conv_b'][j], p['ssd_dt_bias'][j],
                                  p['ssd_a_log'][j], p['ssd_d'][j], p['ssd_norm_g'][j], p['ssd_w_out'][j])
            new['cache_ssd_conv'].append(buf)
            new['state_ssd'].append(s)
        elif mixer == 1:
            y, buf, s = rwkv_mixer(hn, st['cache_rwkv_shift'][j], st['state_rwkv'][j],
                                   p['rwkv_mu'][j], p['rwkv_w_rkv'][j], p['rwkv_w0'][j], p['rwkv_w1'][j],
                                   p['rwkv_w2'][j], p['rwkv_a0'][j], p['rwkv_a1'][j], p['rwkv_a2'][j],
                                   p['rwkv_g1'][j], p['rwkv_g2'][j], p['rwkv_k_k'][j], p['rwkv_k_a'][j],
                                   p['rwkv_r_k'][j], p['rwkv_lnx_g'][j], p['rwkv_lnx_b'][j], p['rwkv_w_o'][j])
            new['cache_rwkv_shift'].append(buf)
            new['state_rwkv'].append(s)
        elif mixer == 2:
            y, buf, s = lru_mixer(hn, st['cache_lru_conv'][j], st['state_lru'][j],
                                  p['lru_w_in'][j], p['lru_conv_w'][j], p['lru_conv_b'][j], p['lru_w_rg'][j],
                                  p['lru_b_rg'][j], p['lru_w_ig'][j], p['lru_b_ig'][j], p['lru_lambda'][j],
                                  p['lru_w_out'][j])
            new['cache_lru_conv'].append(buf)
            new['state_lru'].append(s)
        else:
            y, buf, s = gdn_mixer(hn, st['cache_gdn_conv'][j], st['state_gdn'][j],
                                  p['gdn_w_in'][j], p['gdn_conv_w'][j], p['gdn_a_log'][j], p['gdn_dt_bias'][j],
                                  p['gdn_norm_g'][j], p['gdn_w_out'][j])
            new['cache_gdn_conv'].append(buf)
            new['state_gdn'].append(s)
        x = x + gt1 * y
        hn = _modulated_norm(x, p['norm_g'][i, 1], sh2, sc2)
        x = x + gt2 * moe_ffn(hn, p['moe_w_router'][i], p['moe_b_router'][i], p['moe_w_gu'][i],
                              p['moe_b_gu'][i], p['moe_w_down'][i], p['moe_b_down'][i])
    out = rmsnorm(x, p['final_g'])
    return out, {key: jnp.stack(vals) for key, vals in new.items()}


def setup_inputs(seed: int = 0) -> dict:
    key = jax.random.key(seed)
    ks = iter(jax.random.split(key, 96))
    f32 = jnp.float32
    D = D_MODEL

    def nrm(shape, scale=1.0):
        return jax.random.normal(next(ks), shape, f32) * scale

    def unif(shape, lo, hi):
        return jax.random.uniform(next(ks), shape, f32, lo, hi)

    def dt_bias_init(shape):
        dt = jnp.exp(unif(shape, math.log(1e-3), math.log(1e-1)))
        return dt + jnp.log(-jnp.expm1(-dt))

    inp = {}
    inp['x_prompt'] = nrm((BATCH, SEQ, D))
    inp['x_sample'] = nrm((DEC_BATCH, DEC_SEQ, D))
    inp['c_prompt'] = nrm((BATCH, D))
    inp['c_sample'] = nrm((DEC_BATCH, D))
    inp['state_ssd'] = nrm((N_SSD, DEC_BATCH, SSD_HEADS, SSD_HEADDIM, SSD_STATE), 0.1)
    inp['cache_ssd_conv'] = nrm((N_SSD, DEC_BATCH, CONV_W - 1, SSD_CONV_DIM))
    inp['state_rwkv'] = nrm((N_RWKV, DEC_BATCH, RWKV_HEADS, RWKV_HEAD, RWKV_HEAD), 0.1)
    inp['cache_rwkv_shift'] = nrm((N_RWKV, DEC_BATCH, D))
    inp['state_lru'] = nrm((N_LRU, DEC_BATCH, LRU_WIDTH))
    inp['cache_lru_conv'] = nrm((N_LRU, DEC_BATCH, CONV_W - 1, LRU_WIDTH))
    inp['state_gdn'] = nrm((N_GDN, DEC_BATCH, GDN_V_HEADS, GDN_HEAD_K, GDN_HEAD_V), 0.5)
    inp['cache_gdn_conv'] = nrm((N_GDN, DEC_BATCH, CONV_W - 1, GDN_CONV_DIM))
    inp['w_ada'] = nrm((DEPTH, D, 6 * D), 0.5 * D ** -0.5)
    inp['b_ada'] = nrm((DEPTH, 6 * D), 0.1)
    inp['norm_g'] = 1.0 + nrm((DEPTH, 2, D), 0.02)
    inp['final_g'] = 1.0 + nrm((D,), 0.02)
    inp['ssd_w_in'] = nrm((N_SSD, D, SSD_IN_DIM), D ** -0.5)
    inp['ssd_conv_w'] = nrm((N_SSD, CONV_W, SSD_CONV_DIM), CONV_W ** -0.5)
    inp['ssd_conv_b'] = nrm((N_SSD, SSD_CONV_DIM), 0.02)
    inp['ssd_dt_bias'] = dt_bias_init((N_SSD, SSD_HEADS))
    inp['ssd_a_log'] = jnp.log(unif((N_SSD, SSD_HEADS), 1.0, 16.0))
    inp['ssd_d'] = 1.0 + nrm((N_SSD, SSD_HEADS), 0.1)
    inp['ssd_norm_g'] = 1.0 + nrm((N_SSD, SSD_D_INNER), 0.02)
    inp['ssd_w_out'] = nrm((N_SSD, SSD_D_INNER, D), SSD_D_INNER ** -0.5)
    inp['rwkv_mu'] = unif((N_RWKV, 6, D), 0.0, 1.0)
    inp['rwkv_w_rkv'] = nrm((N_RWKV, 3, D, D), D ** -0.5)
    inp['rwkv_w0'] = unif((N_RWKV, D), -6.0, 0.0)
    inp['rwkv_w1'] = nrm((N_RWKV, D, RWKV_DECAY_LORA), D ** -0.5)
    inp['rwkv_w2'] = nrm((N_RWKV, RWKV_DECAY_LORA, D), 0.5 * RWKV_DECAY_LORA ** -0.5)
    inp['rwkv_a0'] = nrm((N_RWKV, D), 0.1)
    inp['rwkv_a1'] = nrm((N_RWKV, D, RWKV_AAA_LORA), D ** -0.5)
    inp['rwkv_a2'] = nrm((N_RWKV, RWKV_AAA_LORA, D), RWKV_AAA_LORA ** -0.5)
    inp['rwkv_g1'] = nrm((N_RWKV, D, RWKV_GATE_LORA), D ** -0.5)
    inp['rwkv_g2'] = nrm((N_RWKV, RWKV_GATE_LORA, D), RWKV_GATE_LORA ** -0.5)
    inp['rwkv_k_k'] = 0.85 + nrm((N_RWKV, D), 0.05)
    inp['rwkv_k_a'] = 1.0 + nrm((N_RWKV, D), 0.05)
    inp['rwkv_r_k'] = nrm((N_RWKV, RWKV_HEADS, RWKV_HEAD), 0.1)
    inp['rwkv_lnx_g'] = 1.0 + nrm((N_RWKV, D), 0.02)
    inp['rwkv_lnx_b'] = nrm((N_RWKV, D), 0.02)
    inp['rwkv_w_o'] = nrm((N_RWKV, D, D), D ** -0.5)
    inp['lru_w_in'] = nrm((N_LRU, D, 2 * LRU_WIDTH), D ** -0.5)
    inp['lru_conv_w'] = nrm((N_LRU, CONV_W, LRU_WIDTH), CONV_W ** -0.5)
    inp['lru_conv_b'] = nrm((N_LRU, LRU_WIDTH), 0.02)
    inp['lru_w_rg'] = nrm((N_LRU, LRU_BLOCKS, LRU_BLOCK, LRU_BLOCK), LRU_BLOCK ** -0.5)
    inp['lru_b_rg'] = nrm((N_LRU, LRU_BLOCKS, LRU_BLOCK), 0.02)
    inp['lru_w_ig'] = nrm((N_LRU, LRU_BLOCKS, LRU_BLOCK, LRU_BLOCK), LRU_BLOCK ** -0.5)
    inp['lru_b_ig'] = nrm((N_LRU, LRU_BLOCKS, LRU_BLOCK), 0.02)
    radius = unif((N_LRU, LRU_WIDTH), 0.9, 0.999) ** (1.0 / LRU_C)
    inp['lru_lambda'] = jnp.log(radius) - jnp.log1p(-radius)
    inp['lru_w_out'] = nrm((N_LRU, LRU_WIDTH, D), LRU_WIDTH ** -0.5)
    inp['gdn_w_in'] = nrm((N_GDN, D, GDN_IN_DIM), D ** -0.5)
    inp['gdn_conv_w'] = nrm((N_GDN, CONV_W, GDN_CONV_DIM), CONV_W ** -0.5)
    inp['gdn_a_log'] = jnp.log(unif((N_GDN, GDN_V_HEADS), 1.0, 16.0))
    inp['gdn_dt_bias'] = dt_bias_init((N_GDN, GDN_V_HEADS))
    inp['gdn_norm_g'] = 1.0 + nrm((N_GDN, GDN_HEAD_V), 0.02)
    inp['gdn_w_out'] = nrm((N_GDN, GDN_VAL_DIM, D), GDN_VAL_DIM ** -0.5)
    inp['moe_w_router'] = nrm((DEPTH, D, N_EXPERTS), D ** -0.5)
    inp['moe_b_router'] = nrm((DEPTH, N_EXPERTS), 0.01)
    inp['moe_w_gu'] = nrm((DEPTH, N_EXPERTS, D, 2 * D_FF), D ** -0.5)
    inp['moe_b_gu'] = nrm((DEPTH, N_EXPERTS, 2 * D_FF), 0.02)
    inp['moe_w_down'] = nrm((DEPTH, N_EXPERTS, D_FF, D), D_FF ** -0.5)
    inp['moe_b_down'] = nrm((DEPTH, N_EXPERTS, D), 0.02)
    return inp


def reference(x_prompt, x_sample, c_prompt, c_sample,
              state_ssd, cache_ssd_conv, state_rwkv, cache_rwkv_shift,
              state_lru, cache_lru_conv, state_gdn, cache_gdn_conv,
              w_ada, b_ada, norm_g, final_g,
              ssd_w_in, ssd_conv_w, ssd_conv_b, ssd_dt_bias, ssd_a_log, ssd_d, ssd_norm_g, ssd_w_out,
              rwkv_mu, rwkv_w_rkv, rwkv_w0, rwkv_w1, rwkv_w2, rwkv_a0, rwkv_a1, rwkv_a2, rwkv_g1, rwkv_g2,
              rwkv_k_k, rwkv_k_a, rwkv_r_k, rwkv_lnx_g, rwkv_lnx_b, rwkv_w_o,
              lru_w_in, lru_conv_w, lru_conv_b, lru_w_rg, lru_b_rg, lru_w_ig, lru_b_ig, lru_lambda, lru_w_out,
              gdn_w_in, gdn_conv_w, gdn_a_log, gdn_dt_bias, gdn_norm_g, gdn_w_out,
              moe_w_router, moe_b_router, moe_w_gu, moe_b_gu, moe_w_down, moe_b_down):
    p = dict(w_ada=w_ada, b_ada=b_ada, norm_g=norm_g, final_g=final_g,
             ssd_w_in=ssd_w_in, ssd_conv_w=ssd_conv_w, ssd_conv_b=ssd_conv_b, ssd_dt_bias=ssd_dt_bias,
             ssd_a_log=ssd_a_log, ssd_d=ssd_d, ssd_norm_g=ssd_norm_g, ssd_w_out=ssd_w_out,
             rwkv_mu=rwkv_mu, rwkv_w_rkv=rwkv_w_rkv, rwkv_w0=rwkv_w0, rwkv_w1=rwkv_w1, rwkv_w2=rwkv_w2,
             rwkv_a0=rwkv_a0, rwkv_a1=rwkv_a1, rwkv_a2=rwkv_a2, rwkv_g1=rwkv_g1, rwkv_g2=rwkv_g2,
             rwkv_k_k=rwkv_k_k, rwkv_k_a=rwkv_k_a, rwkv_r_k=rwkv_r_k, rwkv_lnx_g=rwkv_lnx_g,
             rwkv_lnx_b=rwkv_lnx_b, rwkv_w_o=rwkv_w_o,
             lru_w_in=lru_w_in, lru_conv_w=lru_conv_w, lru_conv_b=lru_conv_b, lru_w_rg=lru_w_rg,
             lru_b_rg=lru_b_rg, lru_w_ig=lru_w_ig, lru_b_ig=lru_b_ig, lru_lambda=lru_lambda, lru_w_out=lru_w_out,
             gdn_w_in=gdn_w_in, gdn_conv_w=gdn_conv_w, gdn_a_log=gdn_a_log, gdn_dt_bias=gdn_dt_bias,
             gdn_norm_g=gdn_norm_g, gdn_w_out=gdn_w_out,
             moe_w_router=moe_w_router, moe_b_router=moe_b_router, moe_w_gu=moe_w_gu, moe_b_gu=moe_b_gu,
             moe_w_down=moe_w_down, moe_b_down=moe_b_down)
    st_prompt = _zero_states(x_prompt.shape[0], x_prompt.dtype)
    st_sample = dict(state_ssd=state_ssd, cache_ssd_conv=cache_ssd_conv, state_rwkv=state_rwkv,
                     cache_rwkv_shift=cache_rwkv_shift, state_lru=state_lru, cache_lru_conv=cache_lru_conv,
                     state_gdn=state_gdn, cache_gdn_conv=cache_gdn_conv)
    y_prompt, new_p = run_trunk(x_prompt, c_prompt, st_prompt, p)
    y_sample, new_s = run_trunk(x_sample, c_sample, st_sample, p)
    return (y_prompt, y_sample,
            new_p['state_ssd'], new_s['state_ssd'],
            new_p['cache_ssd_conv'], new_s['cache_ssd_conv'],
            new_p['state_rwkv'], new_s['state_rwkv'],
            new_p['cache_rwkv_shift'], new_s['cache_rwkv_shift'],
            new_p['state_lru'], new_s['state_lru'],
            new_p['cache_lru_conv'], new_s['cache_lru_conv'],
            new_p['state_gdn'], new_s['state_gdn'],
            new_p['cache_gdn_conv'], new_s['cache_gdn_conv'])
```

```python
import functools
import math

import jax
import jax.numpy as jnp
from jax import lax
from jax.experimental import pallas as pl
from jax.experimental.pallas import tpu as pltpu

F32 = jnp.float32
BF16 = jnp.bfloat16

VMEM_LIMIT_BYTES = 56 * 1024 * 1024
LANES = 128
SUBLANES = 8

D_MODEL = 2048
DEPTH = 4
CONV_W = 4
NORM_EPS = 1e-6
L2_EPS = 1e-6

LRU_BLOCKS = 8
LRU_C = 8.0


def _cparams(*sem):
    return pltpu.CompilerParams(dimension_semantics=sem, vmem_limit_bytes=VMEM_LIMIT_BYTES)


def _sigmoid(x):
    return 1.0 / (1.0 + jnp.exp(-x))


def _silu(x):
    return x * _sigmoid(x)


def _softplus(x):
    return jnp.maximum(x, 0.0) + jnp.log(1.0 + jnp.exp(-jnp.abs(x)))


def _expm1(x):
    u = jnp.exp(x)
    safe = jnp.where(u == 1.0, 2.0, u)
    return jnp.where(u == 1.0, x, (u - 1.0) * x / jnp.log(safe))


def _dot(a, b):
    return jnp.dot(a.astype(BF16), b.astype(BF16), preferred_element_type=F32)


def _dot_nt(a, b):
    return lax.dot_general(a.astype(BF16), b.astype(BF16), (((1,), (1,)), ((), ())),
                           preferred_element_type=F32)


def _dot_tn(a, b):
    return lax.dot_general(a.astype(BF16), b.astype(BF16), (((0,), (0,)), ((), ())),
                           preferred_element_type=F32)


def _split3(a):
    hi = a.astype(BF16)
    r1 = a - hi.astype(F32)
    mid = r1.astype(BF16)
    lo = (r1 - mid.astype(F32)).astype(BF16)
    return hi, mid, lo


def _dot3(a, b):
    ah, am, _ = _split3(a)
    bh, bm, _ = _split3(b)
    out = jnp.dot(ah, bm, preferred_element_type=F32) + jnp.dot(am, bh, preferred_element_type=F32)
    return out + jnp.dot(ah, bh, preferred_element_type=F32)


def _dot6(a, b):
    ah, am, al = _split3(a)
    bh, bm, bl = _split3(b)
    dot = lambda x, y: jnp.dot(x, y, preferred_element_type=F32)
    small = dot(am, bm) + dot(ah, bl) + dot(al, bh)
    return (small + dot(ah, bm) + dot(am, bh)) + dot(ah, bh)


def _mm_kernel(x_ref, w_ref, *rest, has_bias, pre_silu):
    if has_bias:
        b_ref, o_ref, xb_ref = rest
    else:
        o_ref, xb_ref = rest

    @pl.when(pl.program_id(1) == 0)
    def _():
        x = x_ref[...]
        xb_ref[...] = (_silu(x) if pre_silu else x).astype(BF16)

    acc = jnp.dot(xb_ref[...], w_ref[...].astype(BF16), preferred_element_type=F32)
    if has_bias:
        acc = acc + b_ref[...]
    o_ref[...] = acc.astype(o_ref.dtype)


def _pick_tile(n, pref):
    t = min(n, pref)
    while n % t:
        t //= 2
    return t


def matmul(x, w, b=None, *, w_index=(), n_cols=None, col0=0, out_dtype=F32, tm=None, tn=None, pre_silu=False):
    m, k = x.shape
    lead = len(w_index)
    assert w.shape[lead] == k
    n_total = w.shape[lead + 1]
    n = n_total if n_cols is None else n_cols
    if tm is None:
        tm = _pick_tile(m, 1024 if k <= 2048 else 512)
    if tn is None:
        tn = _pick_tile(n, 512)
    assert m % tm == 0 and n % tn == 0 and col0 % tn == 0
    cb = col0 // tn
    widx = tuple(w_index)
    in_specs = [
        pl.BlockSpec((tm, k), lambda i, j: (i, 0)),
        pl.BlockSpec((None,) * lead + (k, tn), lambda i, j: widx + (0, j + cb)),
    ]
    args = [x, w]
    if b is not None:
        bb = b.reshape(b.shape[:lead] + (1, n_total))
        in_specs.append(pl.BlockSpec((None,) * lead + (1, tn), lambda i, j: widx + (0, j + cb)))
        args.append(bb)
    return pl.pallas_call(
        functools.partial(_mm_kernel, has_bias=b is not None, pre_silu=pre_silu),
        grid=(m // tm, n // tn),
        in_specs=in_specs,
        out_specs=pl.BlockSpec((tm, tn), lambda i, j: (i, j)),
        out_shape=jax.ShapeDtypeStruct((m, n), out_dtype),
        scratch_shapes=[pltpu.VMEM((tm, k), BF16)],
        compiler_params=_cparams("parallel", "arbitrary"),
        name="matmul",
    )(*args)


def _rms(x, g):
    return x * lax.rsqrt(jnp.mean(x * x, axis=-1, keepdims=True) + NORM_EPS) * g


def _norm_mod_kernel(x_ref, g_ref, sh_ref, sc_ref, o_ref):
    y = _rms(x_ref[...], g_ref[...])
    o_ref[...] = y * (1.0 + sc_ref[...]) + sh_ref[...]


def _seq_blocks(bsz, seqlen):
    if seqlen >= 256:
        return 1, 256
    return max(1, 128 // seqlen), seqlen


def _mod_spec(bb, col):
    return pl.BlockSpec((bb, 1, D_MODEL), lambda i, j: (i, 0, col))


def norm_mod(x, g, mod, which):
    bsz, seqlen, d = x.shape
    bb, bl = _seq_blocks(bsz, seqlen)
    xspec = pl.BlockSpec((bb, bl, d), lambda i, j: (i, j, 0))
    return pl.pallas_call(
        _norm_mod_kernel,
        grid=(bsz // bb, seqlen // bl),
        in_specs=[xspec, pl.BlockSpec((1, 1, d), lambda i, j: (0, 0, 0)),
                  _mod_spec(bb, 3 * which), _mod_spec(bb, 3 * which + 1)],
        out_specs=xspec,
        out_shape=jax.ShapeDtypeStruct(x.shape, F32),
        compiler_params=_cparams("parallel", "parallel"),
        name="norm_mod",
    )(x, g.reshape(1, 1, d), mod, mod)


def _resid_norm_kernel(x_ref, y_ref, gt_ref, g_ref, sh_ref, sc_ref, xo_ref, ho_ref):
    xn = x_ref[...] + gt_ref[...] * y_ref[...]
    xo_ref[...] = xn
    ho_ref[...] = _rms(xn, g_ref[...]) * (1.0 + sc_ref[...]) + sh_ref[...]


def _rows_view(y, row0, bsz, seqlen, bb, bl):
    d = y.shape[-1]
    assert bb == 1 or bl == seqlen
    assert row0 % (bb * bl) == 0
    base, per_seq = row0 // (bb * bl), seqlen // bl
    return y.reshape(-1, bl, d), pl.BlockSpec((bb, bl, d), lambda i, j: (base + i * per_seq + j, 0, 0))


def resid_norm(x, y, row0, g, mod_gate, which_gate, mod_norm, which_norm):
    bsz, seqlen, d = x.shape
    bb, bl = _seq_blocks(bsz, seqlen)
    xspec = pl.BlockSpec((bb, bl, d), lambda i, j: (i, j, 0))
    y3, yspec = _rows_view(y, row0, bsz, seqlen, bb, bl)
    return pl.pallas_call(
        _resid_norm_kernel,
        grid=(bsz // bb, seqlen // bl),
        in_specs=[xspec, yspec, _mod_spec(bb, 3 * which_gate + 2),
                  pl.BlockSpec((1, 1, d), lambda i, j: (0, 0, 0)),
                  _mod_spec(bb, 3 * which_norm), _mod_spec(bb, 3 * which_norm + 1)],
        out_specs=[xspec, xspec],
        out_shape=[jax.ShapeDtypeStruct(x.shape, F32)] * 2,
        compiler_params=_cparams("parallel", "parallel"),
        name="resid_norm",
    )(x, y3, mod_gate, g.reshape(1, 1, d), mod_norm, mod_norm)


def _resid_final_kernel(x_ref, y_ref, gt_ref, g_ref, o_ref):
    o_ref[...] = _rms(x_ref[...] + gt_ref[...] * y_ref[...], g_ref[...])


def resid_final(x, y, row0, g, mod_gate, which_gate):
    bsz, seqlen, d = x.shape
    bb, bl = _seq_blocks(bsz, seqlen)
    xspec = pl.BlockSpec((bb, bl, d), lambda i, j: (i, j, 0))
    y3, yspec = _rows_view(y, row0, bsz, seqlen, bb, bl)
    return pl.pallas_call(
        _resid_final_kernel,
        grid=(bsz // bb, seqlen // bl),
        in_specs=[xspec, yspec, _mod_spec(bb, 3 * which_gate + 2),
                  pl.BlockSpec((1, 1, d), lambda i, j: (0, 0, 0))],
        out_specs=xspec,
        out_shape=jax.ShapeDtypeStruct(x.shape, F32),
        compiler_params=_cparams("parallel", "parallel"),
        name="resid_final",
    )(x, y3, mod_gate, g.reshape(1, 1, d))


def _shift_time(x, d, fill, axis):
    t = lax.broadcasted_iota(jnp.int32, x.shape, axis)
    return jnp.where(t >= d, pltpu.roll(x, d, axis), fill)


def _conv_silu_free(xp_ref, w_ref, b, tlen):
    acc = None
    for k in range(CONV_W):
        term = xp_ref[:, pl.ds(SUBLANES - (CONV_W - 1) + k, tlen), :] * w_ref[k:k + 1, :]
        acc = term if acc is None else acc + term
    if b is not None:
        acc = acc + b
    return acc


def _load_history(xp_ref, x_ref, buf_ref, tlen):
    j = pl.program_id(1)

    @pl.when(j == 0)
    def _():
        xp_ref[:, SUBLANES - (CONV_W - 1):SUBLANES, :] = buf_ref[...]

    @pl.when(j > 0)
    def _():
        xp_ref[:, SUBLANES - (CONV_W - 1):SUBLANES, :] = xp_ref[:, tlen + SUBLANES - (CONV_W - 1):tlen + SUBLANES, :]

    xp_ref[:, SUBLANES:SUBLANES + tlen, :] = x_ref[...]


def _gelu_tanh(x):
    return 0.5 * x * (1.0 + jnp.tanh(math.sqrt(2.0 / math.pi) * (x + 0.044715 * (x * x * x))))


def _lru_kernel(xb_ref, gb_ref, buf_ref, h0_ref, cw_ref, cb_ref, wrg_ref, brg_ref, wig_ref, big_ref, lam_ref,
                y_ref, nbuf_ref, hfin_ref, xp_ref, h_ref, *, tlen):
    j = pl.program_id(1)
    bb = xb_ref.shape[0]
    width = xb_ref.shape[2]
    blk = width // LRU_BLOCKS

    @pl.when(j == 0)
    def _():
        h_ref[...] = h0_ref[...]

    _load_history(xp_ref, xb_ref, buf_ref, tlen)
    xc = _conv_silu_free(xp_ref, cw_ref, cb_ref[...], tlen)
    x2 = xc.reshape(bb * tlen, width)
    gr, gi = [], []
    for h in range(LRU_BLOCKS):
        xh = x2[:, h * blk:(h + 1) * blk]
        gr.append(_dot(xh, wrg_ref[h]))
        gi.append(_dot(xh, wig_ref[h]))
    gate_r = _sigmoid(jnp.concatenate(gr, axis=-1) + brg_ref[...]).reshape(bb, tlen, width)
    gate_i = _sigmoid(jnp.concatenate(gi, axis=-1) + big_ref[...]).reshape(bb, tlen, width)
    log_a = (-LRU_C) * gate_r * _softplus(-lam_ref[...])
    a = jnp.exp(log_a)
    bx = jnp.sqrt(-_expm1(2.0 * log_a)) * gate_i * xc
    d = 1
    while d < tlen:
        bx = a * _shift_time(bx, d, 0.0, 1) + bx
        a = a * _shift_time(a, d, 1.0, 1)
        d *= 2
    hs = bx + a * h_ref[...]
    h_ref[...] = hs[:, tlen - 1:tlen, :]
    y = hs * _gelu_tanh(gb_ref[...])
    y_ref[...] = y.reshape(bb * tlen, width).astype(y_ref.dtype)

    @pl.when(j == pl.num_programs(1) - 1)
    def _():
        nbuf_ref[...] = xp_ref[:, tlen + SUBLANES - (CONV_W - 1):tlen + SUBLANES, :]
        hfin_ref[...] = hs[:, tlen - 1:tlen, :]


def lru_core(xg, conv_buf, h0, conv_w, conv_b, w_rg, b_rg, w_ig, b_ig, lam):
    bsz, seqlen, w2 = xg.shape
    width = w2 // 2
    bb, tlen = _seq_blocks(bsz, seqlen)
    nj = seqlen // tlen
    row = lambda v: v.reshape(1, 1, width)
    full3 = pl.BlockSpec((1, 1, width), lambda i, j: (0, 0, 0))
    wspec = pl.BlockSpec(w_rg.shape, lambda i, j: (0, 0, 0))
    y, nbuf, hfin = pl.pallas_call(
        functools.partial(_lru_kernel, tlen=tlen),
        grid=(bsz // bb, nj),
        in_specs=[pl.BlockSpec((bb, tlen, width), lambda i, j: (i, j, 0)),
                  pl.BlockSpec((bb, tlen, width), lambda i, j: (i, j, 1)),
                  pl.BlockSpec((bb, CONV_W - 1, width), lambda i, j: (i, 0, 0)),
                  pl.BlockSpec((bb, 1, width), lambda i, j: (i, 0, 0)),
                  pl.BlockSpec((CONV_W, width), lambda i, j: (0, 0)),
                  full3, wspec, full3, wspec, full3, full3],
        out_specs=[pl.BlockSpec((bb * tlen, width), lambda i, j: (i * nj + j, 0)),
                   pl.BlockSpec((bb, CONV_W - 1, width), lambda i, j: (i, 0, 0)),
                   pl.BlockSpec((bb, 1, width), lambda i, j: (i, 0, 0))],
        out_shape=[jax.ShapeDtypeStruct((bsz * seqlen, width), BF16),
                   jax.ShapeDtypeStruct((bsz, CONV_W - 1, width), F32),
                   jax.ShapeDtypeStruct((bsz, 1, width), F32)],
        scratch_shapes=[pltpu.VMEM((bb, tlen + SUBLANES, width), F32), pltpu.VMEM((bb, 1, width), F32)],
        compiler_params=_cparams("parallel", "arbitrary"),
        name="lru_core",
    )(xg, xg, conv_buf, h0.reshape(bsz, 1, width), conv_w, row(conv_b), w_rg, row(b_rg), w_ig, row(b_ig), row(lam))
    return y, nbuf, hfin.reshape(bsz, width)


SSD_HEADDIM = 64
SSD_GROUPS = 8
SSD_HPG = 8
SSD_HEADS = SSD_GROUPS * SSD_HPG
SSD_STATE = 128
SSD_D_INNER = SSD_HEADS * SSD_HEADDIM
SSD_GROUP_W = SSD_HPG * SSD_HEADDIM
SSD_CONV_DIM = SSD_D_INNER + 2 * SSD_GROUPS * SSD_STATE
SSD_QP = 128


def _cumsum_time(x):
    n = x.shape[0]
    d = 1
    while d < n:
        x = x + _shift_time(x, d, 0.0, 0)
        d *= 2
    return x


def _pad_rows(x, rows):
    if x.shape[0] == rows:
        return x
    return jnp.concatenate([x, jnp.zeros((rows - x.shape[0],) + x.shape[1:], x.dtype)], axis=0)


def _lane_half(shape):
    return lax.broadcasted_iota(jnp.int32, shape, 1) < SSD_HEADDIM


def _ssd_kernel(z_ref, x_ref, b_ref, c_ref, dt_ref, bufx_ref, bufb_ref, bufc_ref, s0_ref,
                cwx_ref, cwb_ref, cwc_ref, cbx_ref, cbb_ref, cbc_ref, dtb_ref, alog_ref, dsk_ref, ng_ref,
                y_ref, nbx_ref, nbb_ref, nbc_ref, s_ref, xpx_ref, xpb_ref, xpc_ref, *, q):
    j = pl.program_id(2)
    qp = SSD_QP

    @pl.when(j == 0)
    def _():
        s_ref[...] = s0_ref[...]

    def conv(xp_ref, x_in, buf, cw, cb, nb_ref):
        @pl.when(j == 0)
        def _():
            xp_ref[:, SUBLANES - (CONV_W - 1):SUBLANES, :] = buf[...]

        @pl.when(j > 0)
        def _():
            xp_ref[:, SUBLANES - (CONV_W - 1):SUBLANES, :] = xp_ref[:, q + SUBLANES - (CONV_W - 1):q + SUBLANES, :]

        xp_ref[:, SUBLANES:SUBLANES + q, :] = x_in[...]

        @pl.when(j == pl.num_programs(2) - 1)
        def _():
            nb_ref[...] = xp_ref[:, q + SUBLANES - (CONV_W - 1):q + SUBLANES, :]

        return _pad_rows(_silu(_conv_silu_free(xp_ref, cw, cb[...], q)[0]), qp)

    xs = conv(xpx_ref, x_ref, bufx_ref, cwx_ref, cbx_ref, nbx_ref)
    bm = conv(xpb_ref, b_ref, bufb_ref, cwb_ref, cbb_ref, nbb_ref)
    cm = conv(xpc_ref, c_ref, bufc_ref, cwc_ref, cbc_ref, nbc_ref)

    row = lax.broadcasted_iota(jnp.int32, (qp, LANES), 0)
    dt = jnp.where(row < q, _softplus(_pad_rows(dt_ref[0], qp) + dtb_ref[0]), 0.0)
    a = -jnp.exp(alog_ref[0])
    cum = _cumsum_time(dt * a)
    last = cum[qp - 1:qp, :]
    ecum = jnp.exp(cum)
    toend = jnp.exp(last - cum) * dt
    elast = jnp.exp(last)
    cum_t = cum.T
    dt_t = dt.T

    causal = lax.broadcasted_iota(jnp.int32, (qp, qp), 0) >= lax.broadcasted_iota(jnp.int32, (qp, qp), 1)
    cb = _dot_nt(cm, bm)
    lane_lo = _lane_half((qp, LANES))
    row_lo = lax.broadcasted_iota(jnp.int32, (LANES, SSD_STATE), 0) < SSD_HEADDIM

    def col(v, h):
        return jnp.broadcast_to(v[:, h:h + 1], (qp, LANES))

    ys = []
    for p in range(SSD_HPG // 2):
        h0, h1 = 2 * p, 2 * p + 1
        xp = xs[:, p * LANES:(p + 1) * LANES]
        intra = []
        for h in (h0, h1):
            seg = col(cum, h) - cum_t[h:h + 1, :]
            m = cb * jnp.exp(jnp.where(causal, seg, -jnp.inf)) * dt_t[h:h + 1, :]
            intra.append(_dot(m, xp))
        y = jnp.where(lane_lo, intra[0], intra[1])
        s_pair = s_ref[0, h0:h0 + 2].reshape(2 * SSD_HEADDIM, SSD_STATE)
        y = y + _dot_nt(cm, s_pair) * jnp.where(lane_lo, col(ecum, h0), col(ecum, h1))
        y = y + dsk_ref[:, p * LANES:(p + 1) * LANES] * xp
        ys.append(y)
        xw = xp * jnp.where(lane_lo, col(toend, h0), col(toend, h1))
        el = jnp.where(row_lo, jnp.broadcast_to(elast[:, h0:h0 + 1], (LANES, SSD_STATE)),
                       jnp.broadcast_to(elast[:, h1:h1 + 1], (LANES, SSD_STATE)))
        s_new = s_pair * el + _dot_tn(xw, bm)
        s_ref[0, h0:h0 + 2] = s_new.reshape(2, SSD_HEADDIM, SSD_STATE)

    yg = jnp.concatenate(ys, axis=-1)[:q] * _silu(z_ref[0])
    yg = yg * lax.rsqrt(jnp.mean(yg * yg, axis=-1, keepdims=True) + NORM_EPS) * ng_ref[...]
    y_ref[...] = yg.astype(y_ref.dtype)


def ssd_core(zxbc, dt_raw, conv_buf, s0, conv_w, conv_b, dt_bias, a_log, d_skip, norm_g):
    bsz, seqlen, _ = zxbc.shape
    q = min(seqlen, SSD_QP)
    nc = seqlen // q
    g8 = SSD_GROUPS

    def grp_pad(v):
        return jnp.pad(v.reshape(g8, 1, SSD_HPG), ((0, 0), (0, 0), (0, LANES - SSD_HPG)))

    d_exp = jnp.repeat(d_skip, SSD_HEADDIM).reshape(1, SSD_D_INNER)
    cb2 = conv_b.reshape(1, SSD_CONV_DIM)
    xoff = SSD_D_INNER // SSD_GROUP_W
    boff = 2 * SSD_D_INNER // SSD_STATE
    coff = boff + SSD_GROUPS
    cboff = SSD_D_INNER // SSD_STATE
    ccoff = cboff + SSD_GROUPS
    seq3 = lambda w, off: pl.BlockSpec((1, q, w), lambda b, g, j: (b, j, g + off))
    buf3 = lambda w, off: pl.BlockSpec((1, CONV_W - 1, w), lambda b, g, j: (b, 0, g + off))
    par2 = lambda r, w, off: pl.BlockSpec((r, w), lambda b, g, j: (0, g + off))
    grp = pl.BlockSpec((1, 1, LANES), lambda b, g, j: (g, 0, 0))
    sspec = pl.BlockSpec((1, SSD_HPG, SSD_HEADDIM, SSD_STATE), lambda b, g, j: (b, g, 0, 0))
    y, nbx, nbb, nbc, sfin = pl.pallas_call(
        functools.partial(_ssd_kernel, q=q),
        grid=(bsz, g8, nc),
        in_specs=[seq3(SSD_GROUP_W, 0), seq3(SSD_GROUP_W, xoff), seq3(SSD_STATE, boff), seq3(SSD_STATE, coff),
                  seq3(LANES, 0),
                  buf3(SSD_GROUP_W, 0), buf3(SSD_STATE, cboff), buf3(SSD_STATE, ccoff), sspec,
                  par2(CONV_W, SSD_GROUP_W, 0), par2(CONV_W, SSD_STATE, cboff), par2(CONV_W, SSD_STATE, ccoff),
                  par2(1, SSD_GROUP_W, 0), par2(1, SSD_STATE, cboff), par2(1, SSD_STATE, ccoff),
                  grp, grp, par2(1, SSD_GROUP_W, 0), par2(1, SSD_GROUP_W, 0)],
        out_specs=[pl.BlockSpec((q, SSD_GROUP_W), lambda b, g, j: (b * nc + j, g)),
                   buf3(SSD_GROUP_W, 0), buf3(SSD_STATE, 0), buf3(SSD_STATE, 0), sspec],
        out_shape=[jax.ShapeDtypeStruct((bsz * seqlen, SSD_D_INNER), F32),
                   jax.ShapeDtypeStruct((bsz, CONV_W - 1, SSD_D_INNER), F32),
                   jax.ShapeDtypeStruct((bsz, CONV_W - 1, SSD_GROUPS * SSD_STATE), F32),
                   jax.ShapeDtypeStruct((bsz, CONV_W - 1, SSD_GROUPS * SSD_STATE), F32),
                   jax.ShapeDtypeStruct(s0.shape, F32)],
        scratch_shapes=[pltpu.VMEM((1, q + SUBLANES, SSD_GROUP_W), F32),
                        pltpu.VMEM((1, q + SUBLANES, SSD_STATE), F32),
                        pltpu.VMEM((1, q + SUBLANES, SSD_STATE), F32)],
        compiler_params=_cparams("parallel", "parallel", "arbitrary"),
        name="ssd_core",
    )(zxbc, zxbc, zxbc, zxbc, dt_raw, conv_buf, conv_buf, conv_buf, s0,
      conv_w, conv_w, conv_w, cb2, cb2, cb2, grp_pad(dt_bias), grp_pad(a_log), d_exp, norm_g.reshape(1, SSD_D_INNER))
    return y, jnp.concatenate([nbx, nbb, nbc], axis=-1), sfin


def ssd_dt_weight(w_in):
    w_dt = w_in[:, SSD_D_INNER + SSD_CONV_DIM:]
    w_dt = w_dt.reshape(-1, SSD_GROUPS, SSD_HPG)
    return jnp.pad(w_dt, ((0, 0), (0, 0), (0, LANES - SSD_HPG))).reshape(-1, SSD_GROUPS * LANES)


RWKV_HEAD = 64
RWKV_HEADS = D_MODEL // RWKV_HEAD
RWKV_GN_EPS = 64e-5
RWKV_CHUNK = 64


def _rwkv_mix_kernel(x_ref, buf_ref, mu_ref, *rest):
    outs, last_ref, carry_ref = rest[:6], rest[6], rest[7]
    j = pl.program_id(1)
    bb, bl, d = x_ref.shape

    @pl.when(j == 0)
    def _():
        carry_ref[...] = buf_ref[...]

    x = x_ref[...]
    t = lax.broadcasted_iota(jnp.int32, x.shape, 1)
    prev = jnp.where(t >= 1, pltpu.roll(x, 1, 1), carry_ref[...])
    carry_ref[...] = x[:, bl - 1:bl, :]
    diff = prev - x
    for s in range(6):
        outs[s][...] = (x + diff * mu_ref[s:s + 1, :]).reshape(bb * bl, d).astype(BF16)

    @pl.when(j == pl.num_programs(1) - 1)
    def _():
        last_ref[...] = x[:, bl - 1:bl, :]


def rwkv_mix(x, shift_buf, mu):
    bsz, seqlen, d = x.shape
    bb, bl = _seq_blocks(bsz, seqlen)
    nj = seqlen // bl
    ospec = pl.BlockSpec((bb * bl, d), lambda i, j: (i * nj + j, 0))
    cspec = pl.BlockSpec((bb, 1, d), lambda i, j: (i, 0, 0))
    outs = pl.pallas_call(
        _rwkv_mix_kernel,
        grid=(bsz // bb, nj),
        in_specs=[pl.BlockSpec((bb, bl, d), lambda i, j: (i, j, 0)), cspec,
                  pl.BlockSpec((6, d), lambda i, j: (0, 0))],
        out_specs=[ospec] * 6 + [cspec],
        out_shape=[jax.ShapeDtypeStruct((bsz * seqlen, d), BF16)] * 6 + [jax.ShapeDtypeStruct((bsz, 1, d), F32)],
        scratch_shapes=[pltpu.VMEM((bb, 1, d), F32)],
        compiler_params=_cparams("parallel", "arbitrary"),
        name="rwkv_mix",
    )(x, shift_buf.reshape(bsz, 1, d), mu)
    return outs[:6], outs[6].reshape(bsz, d)


def _lora_kernel(x_ref, w1_ref, w2_ref, *rest, act, has_bias):
    if has_bias:
        b_ref, o_ref = rest
    else:
        (o_ref,) = rest
    hmid = _dot(x_ref[...], w1_ref[...])
    if act == "tanh":
        hmid = jnp.tanh(hmid)
    elif act == "sigmoid":
        hmid = _sigmoid(hmid)
    out = _dot(hmid, w2_ref[...])
    if has_bias:
        out = out + b_ref[...]
    o_ref[...] = out


def lora(x, w1, w2, bias, act):
    m, k = x.shape
    r = w1.shape[1]
    n = w2.shape[1]
    tm = _pick_tile(m, 512)
    in_specs = [pl.BlockSpec((tm, k), lambda i: (i, 0)), pl.BlockSpec((k, r), lambda i: (0, 0)),
                pl.BlockSpec((r, n), lambda i: (0, 0))]
    args = [x, w1, w2]
    if bias is not None:
        in_specs.append(pl.BlockSpec((1, n), lambda i: (0, 0)))
        args.append(bias.reshape(1, n))
    return pl.pallas_call(
        functools.partial(_lora_kernel, act=act, has_bias=bias is not None),
        grid=(m // tm,),
        in_specs=in_specs,
        out_specs=pl.BlockSpec((tm, n), lambda i: (i, 0)),
        out_shape=jax.ShapeDtypeStruct((m, n), F32),
        compiler_params=_cparams("parallel"),
        name="lora",
    )(*args)


def _half_sum(x, lo):
    s0 = jnp.sum(jnp.where(lo, x, 0.0), axis=-1, keepdims=True)
    s1 = jnp.sum(jnp.where(lo, 0.0, x), axis=-1, keepdims=True)
    return jnp.where(lo, s0, s1)


def _stack_heads(x, lo):
    return jnp.concatenate([jnp.where(lo, x, 0.0), jnp.where(lo, 0.0, x)], axis=0)


def _rwkv_kernel(r_ref, k_ref, v_ref, w_ref, a_ref, g_ref, s0_ref, kk_ref, ka_ref, rk_ref, lg_ref, lb_ref,
                 y_ref, sfin_ref, st_ref, *, c):
    j = pl.program_id(2)
    hd = RWKV_HEAD
    lo = lax.broadcasted_iota(jnp.int32, (c, LANES), 1) < hd
    zero = jnp.zeros((hd, hd), F32)

    @pl.when(j == 0)
    def _():
        st_ref[...] = jnp.concatenate([jnp.concatenate([s0_ref[0, 0], zero], axis=1),
                                       jnp.concatenate([zero, s0_ref[0, 1]], axis=1)], axis=0)

    r, k, v = r_ref[0], k_ref[0], v_ref[0]
    lw = -jnp.exp(-_softplus(-w_ref[0]) - 0.5)
    a = _sigmoid(a_ref[0])
    kkr = k * kk_ref[...]
    kk = kkr * lax.rsqrt(_half_sum(kkr * kkr, lo) + L2_EPS)
    k = k * (1.0 + (a - 1.0) * ka_ref[...])
    cum = _cumsum_time(lw)
    gam = jnp.exp(cum)
    inv = jnp.exp(-cum)
    glast = gam[c - 1:c, :]
    ka_s = _stack_heads(jnp.exp(cum - lw) * kk, lo)
    al_s = _stack_heads(kk * a * inv, lo)
    k_s = _stack_heads(k * inv, lo)
    r_s = _stack_heads(r * gam, lo)
    v_s = _stack_heads(v, lo)

    n = 2 * c
    ti = lax.broadcasted_iota(jnp.int32, (n, n), 0) % c
    si = lax.broadcasted_iota(jnp.int32, (n, n), 1) % c
    strict, incl = ti > si, ti >= si
    st = st_ref[...]
    a_mat = jnp.where(strict, _dot_nt(ka_s, al_s), 0.0)
    b_mat = jnp.where(strict, _dot_nt(ka_s, k_s), 0.0)
    d = _dot(_unit_lower_inverse(a_mat, c), _dot_nt(ka_s, st) + _dot(b_mat, v_s))
    o = (_dot_nt(r_s, st) - _dot(jnp.where(incl, _dot_nt(r_s, al_s), 0.0), d)
         + _dot(jnp.where(incl, _dot_nt(r_s, k_s), 0.0), v_s))
    st_ref[...] = st * glast + _dot_tn(v_s, k_s * glast) - _dot_tn(d, al_s * glast)

    o = o[:c] + o[c:]
    mean = _half_sum(o, lo) * (1.0 / hd)
    var = _half_sum(jnp.square(o - mean), lo) * (1.0 / hd)
    o = (o - mean) * lax.rsqrt(var + RWKV_GN_EPS) * lg_ref[...] + lb_ref[...]
    o = o + _half_sum(r * k * rk_ref[...], lo) * v
    y_ref[...] = (o * g_ref[0]).astype(y_ref.dtype)

    @pl.when(j == pl.num_programs(2) - 1)
    def _():
        fin = st_ref[...]
        sfin_ref[0, 0] = fin[:hd, :hd]
        sfin_ref[0, 1] = fin[hd:, hd:]


def rwkv_core(r, k, v, w_raw, a_pre, g, s0, k_k, k_a, r_k, lnx_g, lnx_b):
    bsz, seqlen, d = r.shape
    c = min(seqlen, RWKV_CHUNK)
    nc = seqlen // c
    seq = pl.BlockSpec((1, c, LANES), lambda b, p, j: (b, j, p))
    par = pl.BlockSpec((1, LANES), lambda b, p, j: (0, p))
    sspec = pl.BlockSpec((1, 2, RWKV_HEAD, RWKV_HEAD), lambda b, p, j: (b, p, 0, 0))
    row = lambda t: t.reshape(1, d)
    return pl.pallas_call(
        functools.partial(_rwkv_kernel, c=c),
        grid=(bsz, d // LANES, nc),
        in_specs=[seq] * 6 + [sspec] + [par] * 5,
        out_specs=[pl.BlockSpec((c, LANES), lambda b, p, j: (b * nc + j, p)), sspec],
        out_shape=[jax.ShapeDtypeStruct((bsz * seqlen, d), F32), jax.ShapeDtypeStruct(s0.shape, F32)],
        scratch_shapes=[pltpu.VMEM((LANES, LANES), F32)],
        compiler_params=_cparams("parallel", "parallel", "arbitrary"),
        name="rwkv_core",
    )(r, k, v, w_raw, a_pre, g, s0, row(k_k), row(k_a), row(r_k), row(lnx_g), row(lnx_b))


def rwkv_mixer(hn, shift_buf, s0, mu, w_rkv, w0, w1, w2, a0, a1, a2, g1, g2, k_k, k_a, r_k, lnx_g, lnx_b, w_o):
    bsz, seqlen, d = hn.shape
    (xr, xk, xv, xw, xa, xg), new_shift = rwkv_mix(hn, shift_buf, mu)
    as3 = lambda t: t.reshape(bsz, seqlen, d)
    r, k, v = (as3(matmul(x, w_rkv, w_index=(s,))) for s, x in enumerate((xr, xk, xv)))
    w_raw = as3(lora(xw, w1, w2, w0, "tanh"))
    a_pre = as3(lora(xa, a1, a2, a0, "none"))
    g = as3(lora(xg, g1, g2, None, "sigmoid"))
    y, s_fin = rwkv_core(r, k, v, w_raw, a_pre, g, s0, k_k, k_a, r_k.reshape(-1), lnx_g, lnx_b)
    return matmul(y, w_o), new_shift, s_fin


GDN_K_HEADS = 16
GDN_V_HEADS = 32
GDN_REP = GDN_V_HEADS // GDN_K_HEADS
GDN_HEAD = 128
GDN_KEY_DIM = GDN_K_HEADS * GDN_HEAD
GDN_VAL_DIM = GDN_V_HEADS * GDN_HEAD
GDN_CONV_DIM = 2 * GDN_KEY_DIM + GDN_VAL_DIM
GDN_CP = 64


def _transpose_tile(x):
    r = x.shape[0]
    return _pad_rows(x, LANES).T[:, :r]


def _l2norm(x):
    return x * lax.rsqrt(jnp.sum(x * x, axis=-1, keepdims=True) + L2_EPS)


def _unit_lower_inverse(a_strict, order=None):
    n = a_strict.shape[0]
    order = n if order is None else order
    eye = (lax.broadcasted_iota(jnp.int32, (n, n), 0) == lax.broadcasted_iota(jnp.int32, (n, n), 1)).astype(F32)
    p = -a_strict
    t = eye + p
    d = 2
    while d < order:
        p = _dot3(p, p)
        t = t + _dot3(t, p)
        d *= 2
    return t


def _gdn_kernel(q_ref, k_ref, v_ref, z_ref, ba_ref, bufq_ref, bufk_ref, bufv_ref, s0_ref,
                cwq_ref, cwk_ref, cwv_ref, alog_ref, dtb_ref, ng_ref,
                o_ref, nbq_ref, nbk_ref, nbv_ref, s_ref, xpq_ref, xpk_ref, xpv_ref, *, c):
    j = pl.program_id(2)
    cp = max(c, SUBLANES) if c < GDN_CP else GDN_CP
    cp = 1 << (cp - 1).bit_length()

    @pl.when(j == 0)
    def _():
        s_ref[...] = s0_ref[...]

    def conv(xp_ref, x_in, buf, cw, nb_ref):
        @pl.when(j == 0)
        def _():
            xp_ref[:, SUBLANES - (CONV_W - 1):SUBLANES, :] = buf[...]

        @pl.when(j > 0)
        def _():
            xp_ref[:, SUBLANES - (CONV_W - 1):SUBLANES, :] = xp_ref[:, c + SUBLANES - (CONV_W - 1):c + SUBLANES, :]

        xp_ref[:, SUBLANES:SUBLANES + c, :] = x_in[...]

        @pl.when(j == pl.num_programs(2) - 1)
        def _():
            nb_ref[...] = xp_ref[:, c + SUBLANES - (CONV_W - 1):c + SUBLANES, :]

        return _silu(_conv_silu_free(xp_ref, cw, None, c)[0])

    q = _pad_rows(_l2norm(conv(xpq_ref, q_ref, bufq_ref, cwq_ref, nbq_ref)) * (GDN_HEAD ** -0.5), cp)
    k = _pad_rows(_l2norm(conv(xpk_ref, k_ref, bufk_ref, cwk_ref, nbk_ref)), cp)
    v = _pad_rows(conv(xpv_ref, v_ref, bufv_ref, cwv_ref, nbv_ref), cp)

    ba = _pad_rows(ba_ref[0], cp)
    live = lax.broadcasted_iota(jnp.int32, (cp, LANES), 0) < c
    beta = jnp.where(live, _sigmoid(ba), 0.0)
    g = jnp.where(live, -jnp.exp(alog_ref[0]) * _softplus(ba + dtb_ref[0]), 0.0)
    gc = _cumsum_time(g)
    gc_t = _transpose_tile(gc)
    glast = gc[cp - 1:cp, :]

    ti = lax.broadcasted_iota(jnp.int32, (cp, cp), 0)
    si = lax.broadcasted_iota(jnp.int32, (cp, cp), 1)
    kk = _dot_nt(k, k)
    qk = _dot_nt(q, k)
    for i in range(GDN_REP):
        gcol = jnp.broadcast_to(gc[:, GDN_REP + i:GDN_REP + i + 1], (cp, LANES))
        bcol = jnp.broadcast_to(beta[:, i:i + 1], (cp, LANES))
        gl = glast[:, GDN_REP + i:GDN_REP + i + 1]
        seg = gcol[:, :cp] - gc_t[GDN_REP + i:GDN_REP + i + 1, :]
        dec_incl = jnp.exp(jnp.where(ti >= si, seg, -jnp.inf))
        a_mat = bcol[:, :cp] * kk * jnp.where(ti > si, dec_incl, 0.0)
        kb = k * bcol
        rhs = jnp.concatenate([v[:, i * GDN_HEAD:(i + 1) * GDN_HEAD] * bcol, kb * jnp.exp(gcol)], axis=-1)
        sol = _dot(_unit_lower_inverse(a_mat), rhs)
        u, w = sol[:, :GDN_HEAD], sol[:, GDN_HEAD:]
        s = s_ref[0, i]
        v_new = u - _dot(w, s)
        o = _dot(q * jnp.exp(gcol), s) + _dot(qk * dec_incl, v_new)
        s_ref[0, i] = s * jnp.exp(gl) + _dot_tn(k * jnp.exp(gl - gcol), v_new)
        o = o[:c]
        o = o * lax.rsqrt(jnp.mean(o * o, axis=-1, keepdims=True) + NORM_EPS) * ng_ref[...]
        o = o * _silu(z_ref[0, :, i * GDN_HEAD:(i + 1) * GDN_HEAD])
        o_ref[:, i * GDN_HEAD:(i + 1) * GDN_HEAD] = o.astype(o_ref.dtype)


def gdn_core(qkvz, ba, conv_buf, s0, conv_w, a_log, dt_bias, norm_g):
    bsz, seqlen, _ = qkvz.shape
    c = min(seqlen, GDN_CP)
    nc = seqlen // c
    hd, kh = GDN_HEAD, GDN_K_HEADS
    vw = GDN_REP * hd

    def head_pad(v):
        return jnp.pad(v.reshape(kh, 1, GDN_REP), ((0, 0), (0, 0), (GDN_REP, LANES - 2 * GDN_REP)))

    koff = GDN_KEY_DIM // hd
    voff = 2 * GDN_KEY_DIM // vw
    zoff = GDN_CONV_DIM // vw
    seq3 = lambda w, off: pl.BlockSpec((1, c, w), lambda b, h, j: (b, j, h + off))
    buf3 = lambda w, off: pl.BlockSpec((1, CONV_W - 1, w), lambda b, h, j: (b, 0, h + off))
    par2 = lambda w, off: pl.BlockSpec((CONV_W, w), lambda b, h, j: (0, h + off))
    hp = pl.BlockSpec((1, 1, LANES), lambda b, h, j: (h, 0, 0))
    sspec = pl.BlockSpec((1, GDN_REP, hd, hd), lambda b, h, j: (b, h, 0, 0))
    o, nbq, nbk, nbv, sfin = pl.pallas_call(
        functools.partial(_gdn_kernel, c=c),
        grid=(bsz, kh, nc),
        in_specs=[seq3(hd, 0), seq3(hd, koff), seq3(vw, voff), seq3(vw, zoff), seq3(LANES, 0),
                  buf3(hd, 0), buf3(hd, koff), buf3(vw, voff), sspec,
                  par2(hd, 0), par2(hd, koff), par2(vw, voff), hp, hp,
                  pl.BlockSpec((1, hd), lambda b, h, j: (0, 0))],
        out_specs=[pl.BlockSpec((c, vw), lambda b, h, j: (b * nc + j, h)),
                   buf3(hd, 0), buf3(hd, 0), buf3(vw, 0), sspec],
        out_shape=[jax.ShapeDtypeStruct((bsz * seqlen, GDN_VAL_DIM), F32),
                   jax.ShapeDtypeStruct((bsz, CONV_W - 1, GDN_KEY_DIM), F32),
                   jax.ShapeDtypeStruct((bsz, CONV_W - 1, GDN_KEY_DIM), F32),
                   jax.ShapeDtypeStruct((bsz, CONV_W - 1, GDN_VAL_DIM), F32),
                   jax.ShapeDtypeStruct(s0.shape, F32)],
        scratch_shapes=[pltpu.VMEM((1, c + SUBLANES, hd), F32), pltpu.VMEM((1, c + SUBLANES, hd), F32),
                        pltpu.VMEM((1, c + SUBLANES, vw), F32)],
        compiler_params=_cparams("parallel", "parallel", "arbitrary"),
        name="gdn_core",
    )(qkvz, qkvz, qkvz, qkvz, ba, conv_buf, conv_buf, conv_buf, s0,
      conv_w, conv_w, conv_w, head_pad(a_log), head_pad(dt_bias), norm_g.reshape(1, hd))
    return o, jnp.concatenate([nbq, nbk, nbv], axis=-1), sfin


def gdn_ba_weight(w_in):
    tail = w_in[:, GDN_CONV_DIM + GDN_VAL_DIM:]
    b_w = tail[:, :GDN_V_HEADS].reshape(-1, GDN_K_HEADS, GDN_REP)
    a_w = tail[:, GDN_V_HEADS:].reshape(-1, GDN_K_HEADS, GDN_REP)
    blk = jnp.concatenate([b_w, a_w], axis=-1)
    return jnp.pad(blk, ((0, 0), (0, 0), (0, LANES - 2 * GDN_REP))).reshape(-1, GDN_K_HEADS * LANES)


N_EXPERTS = 32
TOP_K = 4
D_FF = 2048
SWIGLU_LIMIT = 7.0
SWIGLU_ALPHA = 1.702
MOE_ROWS = 256
MOE_TN_UP = 512
MOE_TN_DOWN = 1024


def _router_kernel(x_ref, w_ref, b_ref, e_ref, g_ref):
    logits = _dot6(x_ref[...], w_ref[...]) + b_ref[...]
    lane = lax.broadcasted_iota(jnp.int32, logits.shape, 1)
    logits = jnp.where(lane < N_EXPERTS, logits, -jnp.inf)
    e_out = jnp.zeros(logits.shape, jnp.int32)
    g_out = jnp.zeros(logits.shape, F32)
    top = None
    for kth in range(TOP_K):
        m = jnp.max(logits, axis=-1, keepdims=True)
        idx = jnp.min(jnp.where(logits == m, lane, LANES), axis=-1, keepdims=True)
        top = m if top is None else top
        e_out = jnp.where(lane == kth, idx, e_out)
        g_out = jnp.where(lane == kth, jnp.exp(m - top), g_out)
        logits = jnp.where(lane == idx, -jnp.inf, logits)
    e_ref[...] = e_out
    g_ref[...] = g_out / jnp.sum(g_out, axis=-1, keepdims=True)


def moe_router(x, w_router, b_router, layer):
    n_tok, d = x.shape
    tm = _pick_tile(n_tok, 512)
    w = jnp.pad(w_router[layer], ((0, 0), (0, LANES - N_EXPERTS)))
    b = jnp.pad(b_router[layer], (0, LANES - N_EXPERTS)).reshape(1, LANES)
    e, g = pl.pallas_call(
        _router_kernel,
        grid=(n_tok // tm,),
        in_specs=[pl.BlockSpec((tm, d), lambda i: (i, 0)), pl.BlockSpec((d, LANES), lambda i: (0, 0)),
                  pl.BlockSpec((1, LANES), lambda i: (0, 0))],
        out_specs=[pl.BlockSpec((tm, LANES), lambda i: (i, 0))] * 2,
        out_shape=[jax.ShapeDtypeStruct((n_tok, LANES), jnp.int32), jax.ShapeDtypeStruct((n_tok, LANES), F32)],
        compiler_params=_cparams("parallel"),
        name="moe_router",
    )(x, w, b)
    return e[:, :TOP_K], g[:, :TOP_K]


def moe_slots(top_e):
    n_pairs = top_e.size
    flat_e = top_e.reshape(-1)
    onehot = (flat_e[:, None] == jnp.arange(N_EXPERTS, dtype=jnp.int32)[None, :]).astype(jnp.int32)
    csum = jnp.cumsum(onehot, axis=0)
    rank = jnp.take_along_axis(csum, flat_e[:, None], axis=1)[:, 0] - 1
    nb_e = (csum[-1] + MOE_ROWS - 1) // MOE_ROWS
    bend = jnp.cumsum(nb_e)
    bstart = bend - nb_e
    slot = bstart[flat_e] * MOE_ROWS + rank
    n_blocks = n_pairs // MOE_ROWS + N_EXPERTS
    slot_tok = jnp.zeros((n_blocks * MOE_ROWS,), jnp.int32).at[slot].set(
        jnp.arange(n_pairs, dtype=jnp.int32) // TOP_K)
    return slot.reshape(top_e.shape), slot_tok, nb_e, bstart, bend, n_blocks


def moe_items(nb_e, bstart, bend, n_blocks, nt):
    i = jnp.arange(n_blocks * nt, dtype=jnp.int32)
    used = bend[-1]
    e_i = jnp.minimum(jnp.searchsorted(bend * nt, i, side="right").astype(jnp.int32), N_EXPERTS - 1)
    local = i - bstart[e_i] * nt
    nbe = jnp.maximum(nb_e[e_i], 1)
    valid = i < used * nt
    spare = i - used * nt
    blk = jnp.where(valid, bstart[e_i] + local % nbe, used + spare // nt)
    n_out = jnp.where(valid, local // nbe, spare % nt)
    e_last = jnp.max(jnp.where(nb_e > 0, jnp.arange(N_EXPERTS, dtype=jnp.int32), 0))
    e_w = jnp.where(valid, e_i, e_last)
    n_w = jnp.where(valid, local // nbe, nt - 1)
    first = (valid & (local % nbe == 0)).astype(jnp.int32)
    return blk.astype(jnp.int32), e_w.astype(jnp.int32), n_w.astype(jnp.int32), n_out.astype(jnp.int32), first, \
        valid.astype(jnp.int32)


def _gmm_kernel(blk_ref, ew_ref, nw_ref, no_ref, first_ref, valid_ref, x_ref, *rest, glu):
    i = pl.program_id(0)
    if glu:
        wg_ref, wu_ref, bg_ref, bu_ref, o_ref, wgb_ref, wub_ref = rest
    else:
        w_ref, b_ref, o_ref, wb_ref = rest

    @pl.when(first_ref[i] == 1)
    def _():
        if glu:
            wgb_ref[...] = wg_ref[...].astype(BF16)
            wub_ref[...] = wu_ref[...].astype(BF16)
        else:
            wb_ref[...] = w_ref[...].astype(BF16)

    @pl.when(valid_ref[i] == 1)
    def _():
        x = x_ref[...]
        if glu:
            gate = jnp.dot(x, wgb_ref[...], preferred_element_type=F32) + bg_ref[...]
            up = jnp.dot(x, wub_ref[...], preferred_element_type=F32) + bu_ref[...]
            gate = jnp.minimum(gate, SWIGLU_LIMIT)
            up = jnp.clip(up, -SWIGLU_LIMIT, SWIGLU_LIMIT)
            out = (up + 1.0) * (gate * _sigmoid(SWIGLU_ALPHA * gate))
        else:
            out = jnp.dot(x, wb_ref[...], preferred_element_type=F32) + b_ref[...]
        o_ref[...] = out.astype(o_ref.dtype)

    @pl.when(valid_ref[i] == 0)
    def _():
        o_ref[...] = jnp.zeros(o_ref.shape, o_ref.dtype)


def grouped_matmul(xs, w, b, layer, items, *, glu):
    n_slots, k = xs.shape
    n_out = w.shape[-1] // 2 if glu else w.shape[-1]
    tn = MOE_TN_UP if glu else MOE_TN_DOWN
    n_items = items[0].shape[0]
    b4 = b.reshape(b.shape[0], b.shape[1], 1, b.shape[2])
    wspec = lambda off: pl.BlockSpec((None, None, k, tn),
                                     lambda i, blk, ew, nw, no, fi, va: (layer, ew[i], 0, nw[i] + off))
    bspec = lambda off: pl.BlockSpec((None, None, 1, tn),
                                     lambda i, blk, ew, nw, no, fi, va: (layer, ew[i], 0, nw[i] + off))
    xspec = pl.BlockSpec((MOE_ROWS, k), lambda i, blk, ew, nw, no, fi, va: (blk[i], 0))
    ospec = pl.BlockSpec((MOE_ROWS, tn), lambda i, blk, ew, nw, no, fi, va: (blk[i], no[i]))
    if glu:
        half = n_out // tn
        in_specs = [xspec, wspec(0), wspec(half), bspec(0), bspec(half)]
        args = (xs, w, w, b4, b4)
        scratch = [pltpu.VMEM((k, tn), BF16)] * 2
    else:
        in_specs = [xspec, wspec(0), bspec(0)]
        args = (xs, w, b4)
        scratch = [pltpu.VMEM((k, tn), BF16)]
    return pl.pallas_call(
        functools.partial(_gmm_kernel, glu=glu),
        grid_spec=pltpu.PrefetchScalarGridSpec(
            num_scalar_prefetch=6, grid=(n_items,), in_specs=in_specs, out_specs=ospec, scratch_shapes=scratch),
        out_shape=jax.ShapeDtypeStruct((n_slots, n_out), BF16 if glu else F32),
        compiler_params=_cparams("arbitrary"),
        name="moe_up" if glu else "moe_down",
    )(*items, *args)


def moe_ffn(hn, layer, w_router, b_router, w_gu, b_gu, w_down, b_down):
    top_e, gates = moe_router(hn, w_router, b_router, layer)
    pair_slot, slot_tok, nb_e, bstart, bend, n_blocks = moe_slots(top_e)
    xs = jnp.take(hn.astype(BF16), slot_tok, axis=0)
    hmid = grouped_matmul(xs, w_gu, b_gu, layer, moe_items(nb_e, bstart, bend, n_blocks, D_FF // MOE_TN_UP), glu=True)
    ys = grouped_matmul(hmid, w_down, b_down, layer,
                        moe_items(nb_e, bstart, bend, n_blocks, D_MODEL // MOE_TN_DOWN), glu=False)
    yk = jnp.take(ys, pair_slot.reshape(-1), axis=0).reshape(hn.shape[0], TOP_K, hn.shape[1])
    return jnp.sum(yk * gates[:, :, None], axis=1)


def ssd_mixer(hn, conv_buf, s0, w_in, conv_w, conv_b, dt_bias, a_log, d_skip, norm_g, w_out):
    bsz, seqlen, d = hn.shape
    h2 = hn.reshape(bsz * seqlen, d)
    zxbc = matmul(h2, w_in, w_index=(0,), n_cols=SSD_D_INNER + SSD_CONV_DIM).reshape(bsz, seqlen, -1)
    dt_raw = matmul(h2, ssd_dt_weight(w_in[0])).reshape(bsz, seqlen, -1)
    y, new_buf, s_fin = ssd_core(zxbc, dt_raw, conv_buf, s0, conv_w[0], conv_b[0], dt_bias[0], a_log[0], d_skip[0],
                                 norm_g[0])
    return matmul(y, w_out, w_index=(0,)), new_buf, s_fin


def lru_mixer(hn, conv_buf, h0, w_in, conv_w, conv_b, w_rg, b_rg, w_ig, b_ig, lam, w_out):
    bsz, seqlen, d = hn.shape
    xg = matmul(hn.reshape(bsz * seqlen, d), w_in, w_index=(0,)).reshape(bsz, seqlen, -1)
    y, new_buf, h_fin = lru_core(xg, conv_buf, h0, conv_w[0], conv_b[0], w_rg[0], b_rg[0].reshape(-1), w_ig[0],
                                 b_ig[0].reshape(-1), lam[0])
    return matmul(y, w_out, w_index=(0,)), new_buf, h_fin


def gdn_mixer(hn, conv_buf, s0, w_in, conv_w, a_log, dt_bias, norm_g, w_out):
    bsz, seqlen, d = hn.shape
    h2 = hn.reshape(bsz * seqlen, d)
    qkvz = matmul(h2, w_in, w_index=(0,), n_cols=GDN_CONV_DIM + GDN_VAL_DIM).reshape(bsz, seqlen, -1)
    ba = matmul(h2, gdn_ba_weight(w_in[0])).reshape(bsz, seqlen, -1)
    o, new_buf, s_fin = gdn_core(qkvz, ba, conv_buf, s0, conv_w[0], a_log[0], dt_bias[0], norm_g[0])
    return matmul(o, w_out, w_index=(0,)), new_buf, s_fin


def kernel(x_prompt, x_sample, c_prompt, c_sample, state_ssd, cache_ssd_conv, state_rwkv, cache_rwkv_shift, state_lru, cache_lru_conv, state_gdn, cache_gdn_conv, w_ada, b_ada, norm_g, final_g, ssd_w_in, ssd_conv_w, ssd_conv_b, ssd_dt_bias, ssd_a_log, ssd_d, ssd_norm_g, ssd_w_out, rwkv_mu, rwkv_w_rkv, rwkv_w0, rwkv_w1, rwkv_w2, rwkv_a0, rwkv_a1, rwkv_a2, rwkv_g1, rwkv_g2, rwkv_k_k, rwkv_k_a, rwkv_r_k, rwkv_lnx_g, rwkv_lnx_b, rwkv_w_o, lru_w_in, lru_conv_w, lru_conv_b, lru_w_rg, lru_b_rg, lru_w_ig, lru_b_ig, lru_lambda, lru_w_out, gdn_w_in, gdn_conv_w, gdn_a_log, gdn_dt_bias, gdn_norm_g, gdn_w_out, moe_w_router, moe_b_router, moe_w_gu, moe_b_gu, moe_w_down, moe_b_down):
    assert DEPTH == 4 and w_ada.shape[0] == DEPTH
    d = D_MODEL
    xs = [x_prompt, x_sample]
    n_seq = [x.shape[0] for x in xs]
    n_rows = [x.shape[0] * x.shape[1] for x in xs]
    row0 = [0, n_rows[0]]

    cond = jnp.concatenate([c_prompt, c_sample], axis=0)
    pad = (-cond.shape[0]) % (2 * SUBLANES)
    cond = jnp.pad(cond, ((0, pad), (0, 0)))
    mods = []
    for i in range(DEPTH):
        m = matmul(cond, w_ada, b_ada, w_index=(i,), pre_silu=True)
        mods.append([m[:n_seq[0]].reshape(n_seq[0], 1, 6 * d),
                     m[n_seq[0]:n_seq[0] + n_seq[1]].reshape(n_seq[1], 1, 6 * d)])

    def zeros_like_state(s, bsz):
        return jnp.zeros((bsz,) + s.shape[2:], F32)

    st = {
        "ssd": [zeros_like_state(state_ssd, n_seq[0]), state_ssd[0]],
        "ssd_conv": [zeros_like_state(cache_ssd_conv, n_seq[0]), cache_ssd_conv[0]],
        "rwkv": [zeros_like_state(state_rwkv, n_seq[0]), state_rwkv[0]],
        "rwkv_shift": [zeros_like_state(cache_rwkv_shift, n_seq[0]), cache_rwkv_shift[0]],
        "lru": [zeros_like_state(state_lru, n_seq[0]), state_lru[0]],
        "lru_conv": [zeros_like_state(cache_lru_conv, n_seq[0]), cache_lru_conv[0]],
        "gdn": [zeros_like_state(state_gdn, n_seq[0]), state_gdn[0]],
        "gdn_conv": [zeros_like_state(cache_gdn_conv, n_seq[0]), cache_gdn_conv[0]],
    }
    new = {}

    hn = [norm_mod(xs[gi], norm_g[0, 0], mods[0][gi], 0) for gi in range(2)]
    outs = None
    for i in range(DEPTH):
        ys = []
        for gi in range(2):
            if i == 0:
                y, buf, s = ssd_mixer(hn[gi], st["ssd_conv"][gi], st["ssd"][gi], ssd_w_in, ssd_conv_w, ssd_conv_b,
                                      ssd_dt_bias, ssd_a_log, ssd_d, ssd_norm_g, ssd_w_out)
                new.setdefault("ssd", []).append(s)
                new.setdefault("ssd_conv", []).append(buf)
            elif i == 1:
                y, buf, s = rwkv_mixer(hn[gi], st["rwkv_shift"][gi], st["rwkv"][gi], rwkv_mu[0], rwkv_w_rkv[0],
                                       rwkv_w0[0], rwkv_w1[0], rwkv_w2[0], rwkv_a0[0], rwkv_a1[0], rwkv_a2[0],
                                       rwkv_g1[0], rwkv_g2[0], rwkv_k_k[0], rwkv_k_a[0], rwkv_r_k[0],
                                       rwkv_lnx_g[0], rwkv_lnx_b[0], rwkv_w_o[0])
                new.setdefault("rwkv", []).append(s)
                new.setdefault("rwkv_shift", []).append(buf)
            elif i == 2:
                y, buf, s = lru_mixer(hn[gi], st["lru_conv"][gi], st["lru"][gi], lru_w_in, lru_conv_w, lru_conv_b,
                                      lru_w_rg, lru_b_rg, lru_w_ig, lru_b_ig, lru_lambda, lru_w_out)
                new.setdefault("lru", []).append(s)
                new.setdefault("lru_conv", []).append(buf)
            else:
                y, buf, s = gdn_mixer(hn[gi], st["gdn_conv"][gi], st["gdn"][gi], gdn_w_in, gdn_conv_w, gdn_a_log,
                                      gdn_dt_bias, gdn_norm_g, gdn_w_out)
                new.setdefault("gdn", []).append(s)
                new.setdefault("gdn_conv", []).append(buf)
            ys.append(y)
        hn2 = []
        for gi in range(2):
            xs[gi], h2 = resid_norm(xs[gi], ys[gi], 0, norm_g[i, 1], mods[i][gi], 0, mods[i][gi], 1)
            hn2.append(h2.reshape(n_rows[gi], d))
        moe = moe_ffn(jnp.concatenate(hn2, axis=0), i, moe_w_router, moe_b_router, moe_w_gu, moe_b_gu, moe_w_down,
                      moe_b_down)
        if i + 1 < DEPTH:
            for gi in range(2):
                xs[gi], hn[gi] = resid_norm(xs[gi], moe, row0[gi], norm_g[i + 1, 0], mods[i][gi], 1,
                                            mods[i + 1][gi], 0)
        else:
            outs = [resid_final(xs[gi], moe, row0[gi], final_g, mods[i][gi], 1) for gi in range(2)]

    res = [outs[0], outs[1]]
    for key in ("ssd", "ssd_conv", "rwkv", "rwkv_shift", "lru", "lru_conv", "gdn", "gdn_conv"):
        res += [new[key][0][None], new[key][1][None]]
    return tuple(res)
```

```python
import functools
import math

import jax
import jax.numpy as jnp
from jax import lax
from jax.experimental import pallas as pl
from jax.experimental.pallas import tpu as pltpu

F32 = jnp.float32
BF16 = jnp.bfloat16

VMEM_LIMIT_BYTES = 56 * 1024 * 1024
LANES = 128
SUBLANES = 8

D_MODEL = 2048
DEPTH = 4
CONV_W = 4
NORM_EPS = 1e-6
L2_EPS = 1e-6

LRU_BLOCKS = 8
LRU_C = 8.0


def _cparams(*sem):
    return pltpu.CompilerParams(dimension_semantics=sem, vmem_limit_bytes=VMEM_LIMIT_BYTES)


def _sigmoid(x):
    return 1.0 / (1.0 + jnp.exp(-x))


def _silu(x):
    return x * _sigmoid(x)


def _softplus(x):
    return jnp.maximum(x, 0.0) + jnp.log(1.0 + jnp.exp(-jnp.abs(x)))


def _expm1(x):
    u = jnp.exp(x)
    safe = jnp.where(u == 1.0, 2.0, u)
    return jnp.where(u == 1.0, x, (u - 1.0) * x / jnp.log(safe))


def _dims(a, ca, cb):
    lead = a.ndim - 2
    batch = tuple(range(lead))
    return (((ca + lead,), (cb + lead,)), (batch, batch))


def _dot(a, b):
    return lax.dot_general(a.astype(BF16), b.astype(BF16), _dims(a, 1, 0), preferred_element_type=F32)


def _dot_nt(a, b):
    return lax.dot_general(a.astype(BF16), b.astype(BF16), _dims(a, 1, 1), preferred_element_type=F32)


def _dot_tn(a, b):
    return lax.dot_general(a.astype(BF16), b.astype(BF16), _dims(a, 0, 0), preferred_element_type=F32)


def _split3(a):
    hi = a.astype(BF16)
    r1 = a - hi.astype(F32)
    mid = r1.astype(BF16)
    lo = (r1 - mid.astype(F32)).astype(BF16)
    return hi, mid, lo


def _dot3(a, b):
    ah, am, _ = _split3(a)
    bh, bm, _ = _split3(b)
    return (_dot(ah, bm) + _dot(am, bh)) + _dot(ah, bh)


def _dot6(a, b):
    ah, am, al = _split3(a)
    bh, bm, bl = _split3(b)
    dot = lambda x, y: jnp.dot(x, y, preferred_element_type=F32)
    small = dot(am, bm) + dot(ah, bl) + dot(al, bh)
    return (small + dot(ah, bm) + dot(am, bh)) + dot(ah, bh)


def _mm_kernel(x_ref, w_ref, *rest, has_bias, pre_silu):
    if has_bias:
        b_ref, o_ref, xb_ref = rest
    else:
        o_ref, xb_ref = rest

    @pl.when(pl.program_id(1) == 0)
    def _():
        x = x_ref[...]
        xb_ref[...] = (_silu(x) if pre_silu else x).astype(BF16)

    acc = jnp.dot(xb_ref[...], w_ref[...].astype(BF16), preferred_element_type=F32)
    if has_bias:
        acc = acc + b_ref[...]
    o_ref[...] = acc.astype(o_ref.dtype)


def _pick_tile(n, pref):
    t = min(n, pref)
    while n % t:
        t //= 2
    return t


def matmul(x, w, b=None, *, w_index=(), n_cols=None, col0=0, out_dtype=F32, tm=None, tn=None, pre_silu=False):
    m, k = x.shape
    lead = len(w_index)
    assert w.shape[lead] == k
    n_total = w.shape[lead + 1]
    n = n_total if n_cols is None else n_cols
    if tm is None:
        tm = _pick_tile(m, 1024 if k <= 2048 else 512)
    if tn is None:
        tn = _pick_tile(n, 512)
    assert m % tm == 0 and n % tn == 0 and col0 % tn == 0
    cb = col0 // tn
    widx = tuple(w_index)
    in_specs = [
        pl.BlockSpec((tm, k), lambda i, j: (i, 0)),
        pl.BlockSpec((None,) * lead + (k, tn), lambda i, j: widx + (0, j + cb)),
    ]
    args = [x, w]
    if b is not None:
        bb = b.reshape(b.shape[:lead] + (1, n_total))
        in_specs.append(pl.BlockSpec((None,) * lead + (1, tn), lambda i, j: widx + (0, j + cb)))
        args.append(bb)
    return pl.pallas_call(
        functools.partial(_mm_kernel, has_bias=b is not None, pre_silu=pre_silu),
        grid=(m // tm, n // tn),
        in_specs=in_specs,
        out_specs=pl.BlockSpec((tm, tn), lambda i, j: (i, j)),
        out_shape=jax.ShapeDtypeStruct((m, n), out_dtype),
        scratch_shapes=[pltpu.VMEM((tm, k), BF16)],
        compiler_params=_cparams("parallel", "arbitrary"),
        name="matmul",
    )(*args)


def _rms(x, g):
    return x * lax.rsqrt(jnp.mean(x * x, axis=-1, keepdims=True) + NORM_EPS) * g


def _norm_mod_kernel(x_ref, g_ref, sh_ref, sc_ref, o_ref):
    y = _rms(x_ref[...], g_ref[...])
    o_ref[...] = y * (1.0 + sc_ref[...]) + sh_ref[...]


def _seq_blocks(bsz, seqlen):
    if seqlen >= 256:
        return 1, 256
    return max(1, 128 // seqlen), seqlen


def _mod_spec(bb, col):
    return pl.BlockSpec((bb, 1, D_MODEL), lambda i, j: (i, 0, col))


def norm_mod(x, g, mod, which):
    bsz, seqlen, d = x.shape
    bb, bl = _seq_blocks(bsz, seqlen)
    xspec = pl.BlockSpec((bb, bl, d), lambda i, j: (i, j, 0))
    return pl.pallas_call(
        _norm_mod_kernel,
        grid=(bsz // bb, seqlen // bl),
        in_specs=[xspec, pl.BlockSpec((1, 1, d), lambda i, j: (0, 0, 0)),
                  _mod_spec(bb, 3 * which), _mod_spec(bb, 3 * which + 1)],
        out_specs=xspec,
        out_shape=jax.ShapeDtypeStruct(x.shape, F32),
        compiler_params=_cparams("parallel", "parallel"),
        name="norm_mod",
    )(x, g.reshape(1, 1, d), mod, mod)


def _resid_norm_kernel(x_ref, y_ref, gt_ref, g_ref, sh_ref, sc_ref, xo_ref, ho_ref):
    xn = x_ref[...] + gt_ref[...] * y_ref[...]
    xo_ref[...] = xn
    ho_ref[...] = _rms(xn, g_ref[...]) * (1.0 + sc_ref[...]) + sh_ref[...]


def _rows_view(y, row0, bsz, seqlen, bb, bl):
    d = y.shape[-1]
    assert bb == 1 or bl == seqlen
    assert row0 % (bb * bl) == 0
    base, per_seq = row0 // (bb * bl), seqlen // bl
    return y.reshape(-1, bl, d), pl.BlockSpec((bb, bl, d), lambda i, j: (base + i * per_seq + j, 0, 0))


def resid_norm(x, y, row0, g, mod_gate, which_gate, mod_norm, which_norm):
    bsz, seqlen, d = x.shape
    bb, bl = _seq_blocks(bsz, seqlen)
    xspec = pl.BlockSpec((bb, bl, d), lambda i, j: (i, j, 0))
    y3, yspec = _rows_view(y, row0, bsz, seqlen, bb, bl)
    return pl.pallas_call(
        _resid_norm_kernel,
        grid=(bsz // bb, seqlen // bl),
        in_specs=[xspec, yspec, _mod_spec(bb, 3 * which_gate + 2),
                  pl.BlockSpec((1, 1, d), lambda i, j: (0, 0, 0)),
                  _mod_spec(bb, 3 * which_norm), _mod_spec(bb, 3 * which_norm + 1)],
        out_specs=[xspec, xspec],
        out_shape=[jax.ShapeDtypeStruct(x.shape, F32)] * 2,
        compiler_params=_cparams("parallel", "parallel"),
        name="resid_norm",
    )(x, y3, mod_gate, g.reshape(1, 1, d), mod_norm, mod_norm)


def _resid_final_kernel(x_ref, y_ref, gt_ref, g_ref, o_ref):
    o_ref[...] = _rms(x_ref[...] + gt_ref[...] * y_ref[...], g_ref[...])


def resid_final(x, y, row0, g, mod_gate, which_gate):
    bsz, seqlen, d = x.shape
    bb, bl = _seq_blocks(bsz, seqlen)
    xspec = pl.BlockSpec((bb, bl, d), lambda i, j: (i, j, 0))
    y3, yspec = _rows_view(y, row0, bsz, seqlen, bb, bl)
    return pl.pallas_call(
        _resid_final_kernel,
        grid=(bsz // bb, seqlen // bl),
        in_specs=[xspec, yspec, _mod_spec(bb, 3 * which_gate + 2),
                  pl.BlockSpec((1, 1, d), lambda i, j: (0, 0, 0))],
        out_specs=xspec,
        out_shape=jax.ShapeDtypeStruct(x.shape, F32),
        compiler_params=_cparams("parallel", "parallel"),
        name="resid_final",
    )(x, y3, mod_gate, g.reshape(1, 1, d))


def _shift_time(x, d, fill, axis):
    t = lax.broadcasted_iota(jnp.int32, x.shape, axis)
    return jnp.where(t >= d, pltpu.roll(x, d, axis), fill)


def _conv_silu_free(xp_ref, w_ref, b, tlen):
    acc = None
    for k in range(CONV_W):
        term = xp_ref[:, pl.ds(SUBLANES - (CONV_W - 1) + k, tlen), :] * w_ref[k:k + 1, :]
        acc = term if acc is None else acc + term
    if b is not None:
        acc = acc + b
    return acc


def _load_history(xp_ref, x_ref, buf_ref, tlen):
    j = pl.program_id(1)

    @pl.when(j == 0)
    def _():
        xp_ref[:, SUBLANES - (CONV_W - 1):SUBLANES, :] = buf_ref[...]

    @pl.when(j > 0)
    def _():
        xp_ref[:, SUBLANES - (CONV_W - 1):SUBLANES, :] = xp_ref[:, tlen + SUBLANES - (CONV_W - 1):tlen + SUBLANES, :]

    xp_ref[:, SUBLANES:SUBLANES + tlen, :] = x_ref[...]


def _gelu_tanh(x):
    return 0.5 * x * (1.0 + jnp.tanh(math.sqrt(2.0 / math.pi) * (x + 0.044715 * (x * x * x))))


def _lru_kernel(xb_ref, gb_ref, buf_ref, h0_ref, cw_ref, cb_ref, wrg_ref, brg_ref, wig_ref, big_ref, lam_ref,
                y_ref, nbuf_ref, hfin_ref, xp_ref, h_ref, *, tlen):
    j = pl.program_id(1)
    bb = xb_ref.shape[0]
    width = xb_ref.shape[2]
    blk = width // LRU_BLOCKS

    @pl.when(j == 0)
    def _():
        h_ref[...] = h0_ref[...]

    _load_history(xp_ref, xb_ref, buf_ref, tlen)
    xc = _conv_silu_free(xp_ref, cw_ref, cb_ref[...], tlen)
    x2 = xc.reshape(bb * tlen, width)
    gr, gi = [], []
    for h in range(LRU_BLOCKS):
        xh = x2[:, h * blk:(h + 1) * blk]
        gr.append(_dot(xh, wrg_ref[h]))
        gi.append(_dot(xh, wig_ref[h]))
    gate_r = _sigmoid(jnp.concatenate(gr, axis=-1) + brg_ref[...]).reshape(bb, tlen, width)
    gate_i = _sigmoid(jnp.concatenate(gi, axis=-1) + big_ref[...]).reshape(bb, tlen, width)
    log_a = (-LRU_C) * gate_r * _softplus(-lam_ref[...])
    a = jnp.exp(log_a)
    bx = jnp.sqrt(-_expm1(2.0 * log_a)) * gate_i * xc
    d = 1
    while d < tlen:
        bx = a * _shift_time(bx, d, 0.0, 1) + bx
        a = a * _shift_time(a, d, 1.0, 1)
        d *= 2
    hs = bx + a * h_ref[...]
    h_ref[...] = hs[:, tlen - 1:tlen, :]
    y = hs * _gelu_tanh(gb_ref[...])
    y_ref[...] = y.reshape(bb * tlen, width).astype(y_ref.dtype)

    @pl.when(j == pl.num_programs(1) - 1)
    def _():
        nbuf_ref[...] = xp_ref[:, tlen + SUBLANES - (CONV_W - 1):tlen + SUBLANES, :]
        hfin_ref[...] = hs[:, tlen - 1:tlen, :]


def lru_core(xg, conv_buf, h0, conv_w, conv_b, w_rg, b_rg, w_ig, b_ig, lam):
    bsz, seqlen, w2 = xg.shape
    width = w2 // 2
    bb, tlen = _seq_blocks(bsz, seqlen)
    nj = seqlen // tlen
    row = lambda v: v.reshape(1, 1, width)
    full3 = pl.BlockSpec((1, 1, width), lambda i, j: (0, 0, 0))
    wspec = pl.BlockSpec(w_rg.shape, lambda i, j: (0, 0, 0))
    y, nbuf, hfin = pl.pallas_call(
        functools.partial(_lru_kernel, tlen=tlen),
        grid=(bsz // bb, nj),
        in_specs=[pl.BlockSpec((bb, tlen, width), lambda i, j: (i, j, 0)),
                  pl.BlockSpec((bb, tlen, width), lambda i, j: (i, j, 1)),
                  pl.BlockSpec((bb, CONV_W - 1, width), lambda i, j: (i, 0, 0)),
                  pl.BlockSpec((bb, 1, width), lambda i, j: (i, 0, 0)),
                  pl.BlockSpec((CONV_W, width), lambda i, j: (0, 0)),
                  full3, wspec, full3, wspec, full3, full3],
        out_specs=[pl.BlockSpec((bb * tlen, width), lambda i, j: (i * nj + j, 0)),
                   pl.BlockSpec((bb, CONV_W - 1, width), lambda i, j: (i, 0, 0)),
                   pl.BlockSpec((bb, 1, width), lambda i, j: (i, 0, 0))],
        out_shape=[jax.ShapeDtypeStruct((bsz * seqlen, width), BF16),
                   jax.ShapeDtypeStruct((bsz, CONV_W - 1, width), F32),
                   jax.ShapeDtypeStruct((bsz, 1, width), F32)],
        scratch_shapes=[pltpu.VMEM((bb, tlen + SUBLANES, width), F32), pltpu.VMEM((bb, 1, width), F32)],
        compiler_params=_cparams("parallel", "arbitrary"),
        name="lru_core",
    )(xg, xg, conv_buf, h0.reshape(bsz, 1, width), conv_w, row(conv_b), w_rg, row(b_rg), w_ig, row(b_ig), row(lam))
    return y, nbuf, hfin.reshape(bsz, width)


SSD_HEADDIM = 64
SSD_GROUPS = 8
SSD_HPG = 8
SSD_HEADS = SSD_GROUPS * SSD_HPG
SSD_STATE = 128
SSD_D_INNER = SSD_HEADS * SSD_HEADDIM
SSD_GROUP_W = SSD_HPG * SSD_HEADDIM
SSD_CONV_DIM = SSD_D_INNER + 2 * SSD_GROUPS * SSD_STATE
SSD_QP = 128


def _cumsum_time(x):
    n = x.shape[0]
    d = 1
    while d < n:
        x = x + _shift_time(x, d, 0.0, 0)
        d *= 2
    return x


def _pad_rows(x, rows):
    if x.shape[0] == rows:
        return x
    return jnp.concatenate([x, jnp.zeros((rows - x.shape[0],) + x.shape[1:], x.dtype)], axis=0)


def _lane_half(shape):
    return lax.broadcasted_iota(jnp.int32, shape, 1) < SSD_HEADDIM


def _ssd_kernel(z_ref, x_ref, b_ref, c_ref, dt_ref, bufx_ref, bufb_ref, bufc_ref, s0_ref,
                cwx_ref, cwb_ref, cwc_ref, cbx_ref, cbb_ref, cbc_ref, dtb_ref, alog_ref, dsk_ref, ng_ref,
                y_ref, nbx_ref, nbb_ref, nbc_ref, s_ref, xpx_ref, xpb_ref, xpc_ref, *, q):
    j = pl.program_id(2)
    qp = max(q, SUBLANES)

    @pl.when(j == 0)
    def _():
        s_ref[...] = s0_ref[...]

    def conv(xp_ref, x_in, buf, cw, cb, nb_ref):
        @pl.when(j == 0)
        def _():
            xp_ref[:, SUBLANES - (CONV_W - 1):SUBLANES, :] = buf[...]

        @pl.when(j > 0)
        def _():
            xp_ref[:, SUBLANES - (CONV_W - 1):SUBLANES, :] = xp_ref[:, q + SUBLANES - (CONV_W - 1):q + SUBLANES, :]

        xp_ref[:, SUBLANES:SUBLANES + q, :] = x_in[...]

        @pl.when(j == pl.num_programs(2) - 1)
        def _():
            nb_ref[...] = xp_ref[:, q + SUBLANES - (CONV_W - 1):q + SUBLANES, :]

        return _pad_rows(_silu(_conv_silu_free(xp_ref, cw, cb[...], q)[0]), qp)

    xs = conv(xpx_ref, x_ref, bufx_ref, cwx_ref, cbx_ref, nbx_ref)
    bm = conv(xpb_ref, b_ref, bufb_ref, cwb_ref, cbb_ref, nbb_ref)
    cm = conv(xpc_ref, c_ref, bufc_ref, cwc_ref, cbc_ref, nbc_ref)

    row = lax.broadcasted_iota(jnp.int32, (qp, LANES), 0)
    dt = jnp.where(row < q, _softplus(_pad_rows(dt_ref[0], qp) + dtb_ref[0]), 0.0)
    a = -jnp.exp(alog_ref[0])
    cum = _cumsum_time(dt * a)
    last = cum[qp - 1:qp, :]
    ecum = jnp.exp(cum)
    toend = jnp.exp(last - cum) * dt
    elast = jnp.exp(last)
    cum_t = _transpose_tile(cum)
    dt_t = _transpose_tile(dt)

    causal = lax.broadcasted_iota(jnp.int32, (qp, qp), 0) >= lax.broadcasted_iota(jnp.int32, (qp, qp), 1)
    cb = _dot_nt(cm, bm)
    lane_lo = _lane_half((qp, LANES))
    row_lo = lax.broadcasted_iota(jnp.int32, (LANES, SSD_STATE), 0) < SSD_HEADDIM

    def col(v, h, n=LANES):
        return jnp.broadcast_to(v[:, h:h + 1], (qp, n))

    npairs = SSD_HPG // 2
    x_pair = jnp.stack([xs[:, p * LANES:(p + 1) * LANES] for p in range(npairs)])
    m_head = jnp.stack([cb * jnp.exp(jnp.where(causal, col(cum, h, qp) - cum_t[h:h + 1, :], -jnp.inf)) * dt_t[h:h + 1, :]
                        for h in range(SSD_HPG)])
    intra = _dot(m_head, jnp.stack([x_pair[h // 2] for h in range(SSD_HPG)]))
    pair_cols = lambda v_: jnp.stack([jnp.where(lane_lo, col(v_, 2 * p), col(v_, 2 * p + 1)) for p in range(npairs)])
    s_pair = s_ref[0].reshape(npairs, 2 * SSD_HEADDIM, SSD_STATE)
    cm_b = jnp.broadcast_to(cm[None], (npairs,) + cm.shape)
    bm_b = jnp.broadcast_to(bm[None], (npairs,) + bm.shape)
    y = jnp.stack([jnp.where(lane_lo, intra[2 * p], intra[2 * p + 1]) for p in range(npairs)])
    y = y + _dot_nt(cm_b, s_pair) * pair_cols(ecum)
    el = jnp.stack([jnp.where(row_lo, jnp.broadcast_to(elast[:, 2 * p:2 * p + 1], (LANES, SSD_STATE)),
                              jnp.broadcast_to(elast[:, 2 * p + 1:2 * p + 2], (LANES, SSD_STATE)))
                    for p in range(npairs)])
    s_new = s_pair * el + _dot_tn(x_pair * pair_cols(toend), bm_b)
    s_ref[0] = s_new.reshape(SSD_HPG, SSD_HEADDIM, SSD_STATE)
    y = jnp.concatenate([y[p] for p in range(npairs)], axis=-1) + dsk_ref[...] * xs

    yg = y[:q] * _silu(z_ref[0])
    yg = yg * lax.rsqrt(jnp.mean(yg * yg, axis=-1, keepdims=True) + NORM_EPS) * ng_ref[...]
    y_ref[...] = yg.astype(y_ref.dtype)


def ssd_core(zxbc, dt_raw, conv_buf, s0, conv_w, conv_b, dt_bias, a_log, d_skip, norm_g):
    bsz, seqlen, _ = zxbc.shape
    q = min(seqlen, SSD_QP)
    nc = seqlen // q
    g8 = SSD_GROUPS

    def grp_pad(v):
        return jnp.pad(v.reshape(g8, 1, SSD_HPG), ((0, 0), (0, 0), (0, LANES - SSD_HPG)))

    d_exp = jnp.repeat(d_skip, SSD_HEADDIM).reshape(1, SSD_D_INNER)
    cb2 = conv_b.reshape(1, SSD_CONV_DIM)
    xoff = SSD_D_INNER // SSD_GROUP_W
    boff = 2 * SSD_D_INNER // SSD_STATE
    coff = boff + SSD_GROUPS
    cboff = SSD_D_INNER // SSD_STATE
    ccoff = cboff + SSD_GROUPS
    seq3 = lambda w, off: pl.BlockSpec((1, q, w), lambda b, g, j: (b, j, g + off))
    buf3 = lambda w, off: pl.BlockSpec((1, CONV_W - 1, w), lambda b, g, j: (b, 0, g + off))
    par2 = lambda r, w, off: pl.BlockSpec((r, w), lambda b, g, j: (0, g + off))
    grp = pl.BlockSpec((1, 1, LANES), lambda b, g, j: (g, 0, 0))
    sspec = pl.BlockSpec((1, SSD_HPG, SSD_HEADDIM, SSD_STATE), lambda b, g, j: (b, g, 0, 0))
    y, nbx, nbb, nbc, sfin = pl.pallas_call(
        functools.partial(_ssd_kernel, q=q),
        grid=(bsz, g8, nc),
        in_specs=[seq3(SSD_GROUP_W, 0), seq3(SSD_GROUP_W, xoff), seq3(SSD_STATE, boff), seq3(SSD_STATE, coff),
                  seq3(LANES, 0),
                  buf3(SSD_GROUP_W, 0), buf3(SSD_STATE, cboff), buf3(SSD_STATE, ccoff), sspec,
                  par2(CONV_W, SSD_GROUP_W, 0), par2(CONV_W, SSD_STATE, cboff), par2(CONV_W, SSD_STATE, ccoff),
                  par2(1, SSD_GROUP_W, 0), par2(1, SSD_STATE, cboff), par2(1, SSD_STATE, ccoff),
                  grp, grp, par2(1, SSD_GROUP_W, 0), par2(1, SSD_GROUP_W, 0)],
        out_specs=[pl.BlockSpec((q, SSD_GROUP_W), lambda b, g, j: (b * nc + j, g)),
                   buf3(SSD_GROUP_W, 0), buf3(SSD_STATE, 0), buf3(SSD_STATE, 0), sspec],
        out_shape=[jax.ShapeDtypeStruct((bsz * seqlen, SSD_D_INNER), F32),
                   jax.ShapeDtypeStruct((bsz, CONV_W - 1, SSD_D_INNER), F32),
                   jax.ShapeDtypeStruct((bsz, CONV_W - 1, SSD_GROUPS * SSD_STATE), F32),
                   jax.ShapeDtypeStruct((bsz, CONV_W - 1, SSD_GROUPS * SSD_STATE), F32),
                   jax.ShapeDtypeStruct(s0.shape, F32)],
        scratch_shapes=[pltpu.VMEM((1, q + SUBLANES, SSD_GROUP_W), F32),
                        pltpu.VMEM((1, q + SUBLANES, SSD_STATE), F32),
                        pltpu.VMEM((1, q + SUBLANES, SSD_STATE), F32)],
        compiler_params=_cparams("parallel", "parallel", "arbitrary"),
        name="ssd_core",
    )(zxbc, zxbc, zxbc, zxbc, dt_raw, conv_buf, conv_buf, conv_buf, s0,
      conv_w, conv_w, conv_w, cb2, cb2, cb2, grp_pad(dt_bias), grp_pad(a_log), d_exp, norm_g.reshape(1, SSD_D_INNER))
    return y, jnp.concatenate([nbx, nbb, nbc], axis=-1), sfin


def ssd_dt_weight(w_in):
    w_dt = w_in[:, SSD_D_INNER + SSD_CONV_DIM:]
    w_dt = w_dt.reshape(-1, SSD_GROUPS, SSD_HPG)
    return jnp.pad(w_dt, ((0, 0), (0, 0), (0, LANES - SSD_HPG))).reshape(-1, SSD_GROUPS * LANES)


RWKV_HEAD = 64
RWKV_HEADS = D_MODEL // RWKV_HEAD
RWKV_GN_EPS = 64e-5
RWKV_CHUNK = 64
RWKV_PAIRS_PER_STEP = 8
RWKV_PAIRS_PER_STEP_SHORT = 16


def _rwkv_mix_kernel(x_ref, buf_ref, mu_ref, *rest):
    outs, last_ref, carry_ref = rest[:6], rest[6], rest[7]
    j = pl.program_id(1)
    bb, bl, d = x_ref.shape

    @pl.when(j == 0)
    def _():
        carry_ref[...] = buf_ref[...]

    x = x_ref[...]
    t = lax.broadcasted_iota(jnp.int32, x.shape, 1)
    prev = jnp.where(t >= 1, pltpu.roll(x, 1, 1), carry_ref[...])
    carry_ref[...] = x[:, bl - 1:bl, :]
    diff = prev - x
    for s in range(6):
        outs[s][...] = (x + diff * mu_ref[s:s + 1, :]).reshape(bb * bl, d).astype(BF16)

    @pl.when(j == pl.num_programs(1) - 1)
    def _():
        last_ref[...] = x[:, bl - 1:bl, :]


def rwkv_mix(x, shift_buf, mu):
    bsz, seqlen, d = x.shape
    bb, bl = _seq_blocks(bsz, seqlen)
    nj = seqlen // bl
    ospec = pl.BlockSpec((bb * bl, d), lambda i, j: (i * nj + j, 0))
    cspec = pl.BlockSpec((bb, 1, d), lambda i, j: (i, 0, 0))
    outs = pl.pallas_call(
        _rwkv_mix_kernel,
        grid=(bsz // bb, nj),
        in_specs=[pl.BlockSpec((bb, bl, d), lambda i, j: (i, j, 0)), cspec,
                  pl.BlockSpec((6, d), lambda i, j: (0, 0))],
        out_specs=[ospec] * 6 + [cspec],
        out_shape=[jax.ShapeDtypeStruct((bsz * seqlen, d), BF16)] * 6 + [jax.ShapeDtypeStruct((bsz, 1, d), F32)],
        scratch_shapes=[pltpu.VMEM((bb, 1, d), F32)],
        compiler_params=_cparams("parallel", "arbitrary"),
        name="rwkv_mix",
    )(x, shift_buf.reshape(bsz, 1, d), mu)
    return outs[:6], outs[6].reshape(bsz, d)


def _lora_kernel(x_ref, w1_ref, w2_ref, *rest, act, has_bias):
    if has_bias:
        b_ref, o_ref = rest
    else:
        (o_ref,) = rest
    hmid = _dot(x_ref[...], w1_ref[...])
    if act == "tanh":
        hmid = jnp.tanh(hmid)
    elif act == "sigmoid":
        hmid = _sigmoid(hmid)
    out = _dot(hmid, w2_ref[...])
    if has_bias:
        out = out + b_ref[...]
    o_ref[...] = out


def lora(x, w1, w2, bias, act):
    m, k = x.shape
    r = w1.shape[1]
    n = w2.shape[1]
    tm = _pick_tile(m, 512)
    in_specs = [pl.BlockSpec((tm, k), lambda i: (i, 0)), pl.BlockSpec((k, r), lambda i: (0, 0)),
                pl.BlockSpec((r, n), lambda i: (0, 0))]
    args = [x, w1, w2]
    if bias is not None:
        in_specs.append(pl.BlockSpec((1, n), lambda i: (0, 0)))
        args.append(bias.reshape(1, n))
    return pl.pallas_call(
        functools.partial(_lora_kernel, act=act, has_bias=bias is not None),
        grid=(m // tm,),
        in_specs=in_specs,
        out_specs=pl.BlockSpec((tm, n), lambda i: (i, 0)),
        out_shape=jax.ShapeDtypeStruct((m, n), F32),
        compiler_params=_cparams("parallel"),
        name="lora",
    )(*args)


def _half_sum(x, lo):
    s0 = jnp.sum(jnp.where(lo, x, 0.0), axis=-1, keepdims=True)
    s1 = jnp.sum(jnp.where(lo, 0.0, x), axis=-1, keepdims=True)
    return jnp.where(lo, s0, s1)


def _stack_heads(x, lo):
    return jnp.concatenate([jnp.where(lo, x, 0.0), jnp.where(lo, 0.0, x)], axis=0)


def _rwkv_kernel(r_ref, k_ref, v_ref, w_ref, a_ref, g_ref, s0_ref, kk_ref, ka_ref, rk_ref, lg_ref, lb_ref,
                 y_ref, sfin_ref, st_ref, *, c):
    j = pl.program_id(2)
    hd = RWKV_HEAD
    npair = r_ref.shape[2] // LANES
    lo = lax.broadcasted_iota(jnp.int32, (c, LANES), 1) < hd
    zero = jnp.zeros((hd, hd), F32)
    pl_ = lambda p: slice(p * LANES, (p + 1) * LANES)

    @pl.when(j == 0)
    def _():
        for p in range(npair):
            st_ref[p] = jnp.concatenate([jnp.concatenate([s0_ref[0, 2 * p], zero], axis=1),
                                         jnp.concatenate([zero, s0_ref[0, 2 * p + 1]], axis=1)], axis=0)

    def pairwise(fn, *xs):
        return jnp.concatenate([fn(*[x[:, pl_(p)] for x in xs]) for p in range(npair)], axis=-1)

    half_sum = lambda x: pairwise(lambda t: _half_sum(t, lo), x)
    r, k, v = r_ref[0], k_ref[0], v_ref[0]
    lw = -jnp.exp(-_softplus(-w_ref[0]) - 0.5)
    a = _sigmoid(a_ref[0])
    kkr = k * kk_ref[...]
    kk = kkr * lax.rsqrt(half_sum(kkr * kkr) + L2_EPS)
    k = k * (1.0 + (a - 1.0) * ka_ref[...])
    cum = _cumsum_time(lw)
    gam = jnp.exp(cum)
    inv = jnp.exp(-cum)
    stack = lambda x: jnp.stack([_stack_heads(x[:, pl_(p)], lo) for p in range(npair)])
    glast = jnp.stack([gam[c - 1:c, pl_(p)] for p in range(npair)])
    ka_s = stack(jnp.exp(cum - lw) * kk)
    al_s = stack(kk * a * inv)
    k_s = stack(k * inv)
    r_s = stack(r * gam)
    v_s = stack(v)

    n = 2 * c
    ti = lax.broadcasted_iota(jnp.int32, (1, n, n), 1) % c
    si = lax.broadcasted_iota(jnp.int32, (1, n, n), 2) % c
    strict, incl = ti > si, ti >= si
    st = st_ref[...]
    a_mat = jnp.where(strict, _dot_nt(ka_s, al_s), 0.0)
    b_mat = jnp.where(strict, _dot_nt(ka_s, k_s), 0.0)
    d = _dot(_unit_lower_inverse(a_mat, c), _dot_nt(ka_s, st) + _dot(b_mat, v_s))
    o = (_dot_nt(r_s, st) - _dot(jnp.where(incl, _dot_nt(r_s, al_s), 0.0), d)
         + _dot(jnp.where(incl, _dot_nt(r_s, k_s), 0.0), v_s))
    st_ref[...] = st * glast + _dot_tn(v_s, k_s * glast) - _dot_tn(d, al_s * glast)

    o = o[:, :c] + o[:, c:]
    o = jnp.concatenate([o[p] for p in range(npair)], axis=-1)
    mean = half_sum(o) * (1.0 / hd)
    var = half_sum(jnp.square(o - mean)) * (1.0 / hd)
    o = (o - mean) * lax.rsqrt(var + RWKV_GN_EPS) * lg_ref[...] + lb_ref[...]
    o = o + half_sum(r * k * rk_ref[...]) * v
    y_ref[...] = (o * g_ref[0]).astype(y_ref.dtype)

    @pl.when(j == pl.num_programs(2) - 1)
    def _():
        for p in range(npair):
            fin = st_ref[p]
            sfin_ref[0, 2 * p] = fin[:hd, :hd]
            sfin_ref[0, 2 * p + 1] = fin[hd:, hd:]


def rwkv_core(r, k, v, w_raw, a_pre, g, s0, k_k, k_a, r_k, lnx_g, lnx_b):
    bsz, seqlen, d = r.shape
    c = min(seqlen, RWKV_CHUNK)
    nc = seqlen // c
    npair = RWKV_PAIRS_PER_STEP if nc > 1 else RWKV_PAIRS_PER_STEP_SHORT
    wid = npair * LANES
    seq = pl.BlockSpec((1, c, wid), lambda b, p, j: (b, j, p))
    par = pl.BlockSpec((1, wid), lambda b, p, j: (0, p))
    sspec = pl.BlockSpec((1, 2 * npair, RWKV_HEAD, RWKV_HEAD), lambda b, p, j: (b, p, 0, 0))
    row = lambda t: t.reshape(1, d)
    return pl.pallas_call(
        functools.partial(_rwkv_kernel, c=c),
        grid=(bsz, d // wid, nc),
        in_specs=[seq] * 6 + [sspec] + [par] * 5,
        out_specs=[pl.BlockSpec((c, wid), lambda b, p, j: (b * nc + j, p)), sspec],
        out_shape=[jax.ShapeDtypeStruct((bsz * seqlen, d), F32), jax.ShapeDtypeStruct(s0.shape, F32)],
        scratch_shapes=[pltpu.VMEM((npair, LANES, LANES), F32)],
        compiler_params=_cparams("parallel", "parallel", "arbitrary"),
        name="rwkv_core",
    )(r, k, v, w_raw, a_pre, g, s0, row(k_k), row(k_a), row(r_k), row(lnx_g), row(lnx_b))


def rwkv_mixer(hn, shift_buf, s0, mu, w_rkv, w0, w1, w2, a0, a1, a2, g1, g2, k_k, k_a, r_k, lnx_g, lnx_b, w_o):
    bsz, seqlen, d = hn.shape
    (xr, xk, xv, xw, xa, xg), new_shift = rwkv_mix(hn, shift_buf, mu)
    as3 = lambda t: t.reshape(bsz, seqlen, d)
    r, k, v = (as3(matmul(x, w_rkv, w_index=(s,))) for s, x in enumerate((xr, xk, xv)))
    w_raw = as3(lora(xw, w1, w2, w0, "tanh"))
    a_pre = as3(lora(xa, a1, a2, a0, "none"))
    g = as3(lora(xg, g1, g2, None, "sigmoid"))
    y, s_fin = rwkv_core(r, k, v, w_raw, a_pre, g, s0, k_k, k_a, r_k.reshape(-1), lnx_g, lnx_b)
    return matmul(y, w_o), new_shift, s_fin


GDN_K_HEADS = 16
GDN_V_HEADS = 32
GDN_REP = GDN_V_HEADS // GDN_K_HEADS
GDN_HEAD = 128
GDN_KEY_DIM = GDN_K_HEADS * GDN_HEAD
GDN_VAL_DIM = GDN_V_HEADS * GDN_HEAD
GDN_CONV_DIM = 2 * GDN_KEY_DIM + GDN_VAL_DIM
GDN_CP = 64
GDN_HEADS_PER_STEP = 8
GDN_HEADS_PER_STEP_SHORT = 16


def _transpose_tile(x):
    r = x.shape[0]
    return _pad_rows(x, LANES).T[:, :r]


def _l2norm(x):
    return x * lax.rsqrt(jnp.sum(x * x, axis=-1, keepdims=True) + L2_EPS)


def _unit_lower_inverse(a_strict, order=None):
    n = a_strict.shape[-1]
    order = n if order is None else order
    eye = (lax.broadcasted_iota(jnp.int32, (n, n), 0) == lax.broadcasted_iota(jnp.int32, (n, n), 1)).astype(F32)
    p = -a_strict
    t = eye + p
    d = 2
    while d < order:
        p = _dot3(p, p)
        t = t + _dot3(t, p)
        d *= 2
    return t


def _gdn_kernel(q_ref, k_ref, v_ref, z_ref, ba_ref, bufq_ref, bufk_ref, bufv_ref, s0_ref,
                cwq_ref, cwk_ref, cwv_ref, alog_ref, dtb_ref, ng_ref,
                o_ref, nbq_ref, nbk_ref, nbv_ref, s_ref, xpq_ref, xpk_ref, xpv_ref, *, c):
    j = pl.program_id(2)
    cp = max(c, SUBLANES) if c < GDN_CP else GDN_CP
    cp = 1 << (cp - 1).bit_length()

    @pl.when(j == 0)
    def _():
        s_ref[...] = s0_ref[...]

    def conv(xp_ref, x_in, buf, cw, nb_ref):
        @pl.when(j == 0)
        def _():
            xp_ref[:, SUBLANES - (CONV_W - 1):SUBLANES, :] = buf[...]

        @pl.when(j > 0)
        def _():
            xp_ref[:, SUBLANES - (CONV_W - 1):SUBLANES, :] = xp_ref[:, c + SUBLANES - (CONV_W - 1):c + SUBLANES, :]

        xp_ref[:, SUBLANES:SUBLANES + c, :] = x_in[...]

        @pl.when(j == pl.num_programs(2) - 1)
        def _():
            nb_ref[...] = xp_ref[:, c + SUBLANES - (CONV_W - 1):c + SUBLANES, :]

        return _silu(_conv_silu_free(xp_ref, cw, None, c)[0])

    nh = q_ref.shape[2] // GDN_HEAD
    q_all = conv(xpq_ref, q_ref, bufq_ref, cwq_ref, nbq_ref)
    k_all = conv(xpk_ref, k_ref, bufk_ref, cwk_ref, nbk_ref)
    v_all = conv(xpv_ref, v_ref, bufv_ref, cwv_ref, nbv_ref)
    nv = nh * GDN_REP
    live = lax.broadcasted_iota(jnp.int32, (cp, LANES), 0) < c
    ti = lax.broadcasted_iota(jnp.int32, (1, cp, cp), 1)
    si = lax.broadcasted_iota(jnp.int32, (1, cp, cp), 2)
    sl = lambda h: slice(h * GDN_HEAD, (h + 1) * GDN_HEAD)
    qs, ks, vs, gcols, bcols, grows = [], [], [], [], [], []
    for hh in range(nh):
        q = _pad_rows(_l2norm(q_all[:, sl(hh)]) * (GDN_HEAD ** -0.5), cp)
        k = _pad_rows(_l2norm(k_all[:, sl(hh)]), cp)
        ba = _pad_rows(ba_ref[0, :, sl(hh)], cp)
        beta = jnp.where(live, _sigmoid(ba), 0.0)
        g = jnp.where(live, -jnp.exp(alog_ref[hh]) * _softplus(ba + dtb_ref[hh]), 0.0)
        gc = _cumsum_time(g)
        gc_t = _transpose_tile(gc)
        for i in range(GDN_REP):
            qs.append(q)
            ks.append(k)
            vs.append(_pad_rows(v_all[:, sl(hh * GDN_REP + i)], cp))
            gcols.append(gc[:, GDN_REP + i:GDN_REP + i + 1])
            bcols.append(beta[:, i:i + 1])
            grows.append(gc_t[GDN_REP + i:GDN_REP + i + 1, :])
    q, k, v = jnp.stack(qs), jnp.stack(ks), jnp.stack(vs)
    gcol, bcol, grow = jnp.stack(gcols), jnp.stack(bcols), jnp.stack(grows)
    glast = gcol[:, cp - 1:cp, :]
    dec_incl = jnp.exp(jnp.where(ti >= si, gcol - grow, -jnp.inf))
    a_mat = bcol * _dot_nt(k, k) * jnp.where(ti > si, dec_incl, 0.0)
    rhs = jnp.concatenate([v * bcol, k * (bcol * jnp.exp(gcol))], axis=-1)
    sol = _dot(_unit_lower_inverse(a_mat), rhs)
    u, w = sol[:, :, :GDN_HEAD], sol[:, :, GDN_HEAD:]
    s = s_ref[0]
    v_new = u - _dot(w, s)
    o = _dot(q * jnp.exp(gcol), s) + _dot(_dot_nt(q, k) * dec_incl, v_new)
    s_ref[0] = s * jnp.exp(glast) + _dot_tn(k * jnp.exp(glast - gcol), v_new)
    o = o[:, :c]
    o = o * lax.rsqrt(jnp.mean(o * o, axis=-1, keepdims=True) + NORM_EPS) * ng_ref[...]
    for vh in range(nv):
        o_ref[:, sl(vh)] = (o[vh] * _silu(z_ref[0, :, sl(vh)])).astype(o_ref.dtype)


def gdn_core(qkvz, ba, conv_buf, s0, conv_w, a_log, dt_bias, norm_g):
    bsz, seqlen, _ = qkvz.shape
    c = min(seqlen, GDN_CP)
    nc = seqlen // c
    kh = GDN_K_HEADS
    nh = GDN_HEADS_PER_STEP if nc > 1 else GDN_HEADS_PER_STEP_SHORT
    hd = nh * GDN_HEAD
    vw = GDN_REP * hd

    def head_pad(v):
        return jnp.pad(v.reshape(kh, 1, GDN_REP), ((0, 0), (0, 0), (GDN_REP, LANES - 2 * GDN_REP)))

    koff = GDN_KEY_DIM // hd
    voff = 2 * GDN_KEY_DIM // vw
    zoff = GDN_CONV_DIM // vw
    seq3 = lambda w, off: pl.BlockSpec((1, c, w), lambda b, h, j: (b, j, h + off))
    buf3 = lambda w, off: pl.BlockSpec((1, CONV_W - 1, w), lambda b, h, j: (b, 0, h + off))
    par2 = lambda w, off: pl.BlockSpec((CONV_W, w), lambda b, h, j: (0, h + off))
    hp = pl.BlockSpec((nh, 1, LANES), lambda b, h, j: (h, 0, 0))
    sspec = pl.BlockSpec((1, nh * GDN_REP, GDN_HEAD, GDN_HEAD), lambda b, h, j: (b, h, 0, 0))
    o, nbq, nbk, nbv, sfin = pl.pallas_call(
        functools.partial(_gdn_kernel, c=c),
        grid=(bsz, kh // nh, nc),
        in_specs=[seq3(hd, 0), seq3(hd, koff), seq3(vw, voff), seq3(vw, zoff), seq3(hd, 0),
                  buf3(hd, 0), buf3(hd, koff), buf3(vw, voff), sspec,
                  par2(hd, 0), par2(hd, koff), par2(vw, voff), hp, hp,
                  pl.BlockSpec((1, GDN_HEAD), lambda b, h, j: (0, 0))],
        out_specs=[pl.BlockSpec((c, vw), lambda b, h, j: (b * nc + j, h)),
                   buf3(hd, 0), buf3(hd, 0), buf3(vw, 0), sspec],
        out_shape=[jax.ShapeDtypeStruct((bsz * seqlen, GDN_VAL_DIM), F32),
                   jax.ShapeDtypeStruct((bsz, CONV_W - 1, GDN_KEY_DIM), F32),
                   jax.ShapeDtypeStruct((bsz, CONV_W - 1, GDN_KEY_DIM), F32),
                   jax.ShapeDtypeStruct((bsz, CONV_W - 1, GDN_VAL_DIM), F32),
                   jax.ShapeDtypeStruct(s0.shape, F32)],
        scratch_shapes=[pltpu.VMEM((1, c + SUBLANES, hd), F32), pltpu.VMEM((1, c + SUBLANES, hd), F32),
                        pltpu.VMEM((1, c + SUBLANES, vw), F32)],
        compiler_params=_cparams("parallel", "parallel", "arbitrary"),
        name="gdn_core",
    )(qkvz, qkvz, qkvz, qkvz, ba, conv_buf, conv_buf, conv_buf, s0,
      conv_w, conv_w, conv_w, head_pad(a_log), head_pad(dt_bias), norm_g.reshape(1, GDN_HEAD))
    return o, jnp.concatenate([nbq, nbk, nbv], axis=-1), sfin


def gdn_ba_weight(w_in):
    tail = w_in[:, GDN_CONV_DIM + GDN_VAL_DIM:]
    b_w = tail[:, :GDN_V_HEADS].reshape(-1, GDN_K_HEADS, GDN_REP)
    a_w = tail[:, GDN_V_HEADS:].reshape(-1, GDN_K_HEADS, GDN_REP)
    blk = jnp.concatenate([b_w, a_w], axis=-1)
    return jnp.pad(blk, ((0, 0), (0, 0), (0, LANES - 2 * GDN_REP))).reshape(-1, GDN_K_HEADS * LANES)


N_EXPERTS = 32
TOP_K = 4
D_FF = 2048
SWIGLU_LIMIT = 7.0
SWIGLU_ALPHA = 1.702
MOE_ROWS = 256
MOE_TN_UP = 512
MOE_TN_DOWN = 1024
MOE_TOKENS_PER_STEP = 256


def _router_kernel(x_ref, w_ref, b_ref, e_ref, g_ref, r_ref, cnt_ref, carry_ref):
    @pl.when(pl.program_id(0) == 0)
    def _():
        carry_ref[...] = jnp.zeros(carry_ref.shape, F32)

    logits = _dot6(x_ref[...], w_ref[...]) + b_ref[...]
    tm = logits.shape[0]
    lane = lax.broadcasted_iota(jnp.int32, logits.shape, 1)
    logits = jnp.where(lane < N_EXPERTS, logits, -jnp.inf)
    e_out = jnp.zeros(logits.shape, jnp.int32)
    g_out = jnp.zeros(logits.shape, F32)
    picked = jnp.zeros(logits.shape, F32)
    top, idxs = None, []
    for kth in range(TOP_K):
        m = jnp.max(logits, axis=-1, keepdims=True)
        idx = jnp.min(jnp.where(logits == m, lane, LANES), axis=-1, keepdims=True)
        top = m if top is None else top
        idxs.append(idx)
        e_out = jnp.where(lane == kth, idx, e_out)
        g_out = jnp.where(lane == kth, jnp.exp(m - top), g_out)
        picked = jnp.where(lane == idx, 1.0, picked)
        logits = jnp.where(lane == idx, -jnp.inf, logits)
    e_ref[...] = e_out
    g_ref[...] = g_out / jnp.sum(g_out, axis=-1, keepdims=True)
    earlier = (lax.broadcasted_iota(jnp.int32, (tm, tm), 0) > lax.broadcasted_iota(jnp.int32, (tm, tm), 1))
    before = carry_ref[...] + _dot(earlier.astype(F32), picked)
    r_out = jnp.zeros(logits.shape, jnp.int32)
    for kth in range(TOP_K):
        rk = jnp.sum(jnp.where(lane == idxs[kth], before, 0.0), axis=-1, keepdims=True)
        r_out = jnp.where(lane == kth, rk.astype(jnp.int32), r_out)
    r_ref[...] = r_out
    carry_ref[...] = carry_ref[...] + jnp.sum(picked, axis=0, keepdims=True)
    cnt_ref[...] = carry_ref[...].astype(jnp.int32)


def moe_router(x, w_router, b_router, layer):
    n_tok, d = x.shape
    tm = _pick_tile(n_tok, 512)
    w = jnp.pad(w_router[layer], ((0, 0), (0, LANES - N_EXPERTS)))
    b = jnp.pad(b_router[layer], (0, LANES - N_EXPERTS)).reshape(1, LANES)
    tok = pl.BlockSpec((tm, LANES), lambda i: (i, 0))
    return pl.pallas_call(
        _router_kernel,
        grid=(n_tok // tm,),
        in_specs=[pl.BlockSpec((tm, d), lambda i: (i, 0)), pl.BlockSpec((d, LANES), lambda i: (0, 0)),
                  pl.BlockSpec((1, LANES), lambda i: (0, 0))],
        out_specs=[tok, tok, tok, pl.BlockSpec((1, LANES), lambda i: (0, 0))],
        out_shape=[jax.ShapeDtypeStruct((n_tok, LANES), jnp.int32), jax.ShapeDtypeStruct((n_tok, LANES), F32),
                   jax.ShapeDtypeStruct((n_tok, LANES), jnp.int32), jax.ShapeDtypeStruct((1, LANES), jnp.int32)],
        scratch_shapes=[pltpu.VMEM((1, LANES), F32)],
        compiler_params=_cparams("arbitrary"),
        name="moe_router",
    )(x, w, b)


def moe_blocks(counts, n_pairs):
    nb_e = (counts + MOE_ROWS - 1) // MOE_ROWS
    bend = jnp.cumsum(nb_e)
    return nb_e, bend - nb_e, bend, n_pairs // MOE_ROWS + N_EXPERTS


def _issue_and_drain(n_tok, start_one, wait_one):
    def start(t, carry):
        for kth in range(TOP_K):
            start_one(t, kth)
        return carry

    def wait(t, carry):
        for _ in range(TOP_K):
            wait_one()
        return carry

    lax.fori_loop(0, n_tok, start, 0)
    lax.fori_loop(0, n_tok, wait, 0)


def _dispatch_kernel(bstart_ref, e_ref, r_ref, x_ref, init_ref, xs_ref, sem):
    del init_ref
    tm = x_ref.shape[0]

    def row_copy(src_row, slot):
        return pltpu.make_async_copy(x_ref.at[pl.ds(src_row, 1)], xs_ref.at[pl.ds(slot, 1)], sem)

    def start_one(t, kth):
        i = t * TOP_K + kth
        slot = bstart_ref[e_ref[0, 0, i]] * MOE_ROWS + r_ref[0, 0, i]
        row_copy(t, slot).start()

    _issue_and_drain(tm, start_one, lambda: row_copy(0, 0).wait())


def moe_dispatch(x_rows, e_t, r_t, bstart, n_slots):
    n_tok, width = x_rows.shape
    n_tiles, _, per_tile = e_t.shape
    tm = per_tile // TOP_K
    smem = pl.BlockSpec((1, 1, per_tile), lambda i, bs: (i, 0, 0), memory_space=pltpu.SMEM)
    return pl.pallas_call(
        _dispatch_kernel,
        grid_spec=pltpu.PrefetchScalarGridSpec(
            num_scalar_prefetch=1, grid=(n_tiles,),
            in_specs=[smem, smem, pl.BlockSpec((tm, width), lambda i, bs: (i, 0)), pl.BlockSpec(memory_space=pl.ANY)],
            out_specs=pl.BlockSpec(memory_space=pl.ANY),
            scratch_shapes=[pltpu.SemaphoreType.DMA(())]),
        out_shape=jax.ShapeDtypeStruct((n_slots, width), x_rows.dtype),
        input_output_aliases={4: 0},
        compiler_params=_cparams("arbitrary"),
        name="moe_dispatch",
    )(bstart, e_t, r_t, x_rows, jnp.zeros((n_slots, width), x_rows.dtype))


def _combine_kernel(bstart_ref, e_ref, r_ref, g_ref, ys_ref, o_ref, buf_ref, sem):
    tm = o_ref.shape[0]

    def row_copy(slot, dst_row):
        return pltpu.make_async_copy(ys_ref.at[pl.ds(slot, 1)], buf_ref.at[pl.ds(dst_row, 1)], sem)

    def start_one(t, kth):
        i = t * TOP_K + kth
        slot = bstart_ref[e_ref[0, 0, i]] * MOE_ROWS + r_ref[0, 0, i]
        row_copy(slot, kth * tm + t).start()

    _issue_and_drain(tm, start_one, lambda: row_copy(0, 0).wait())
    acc = None
    for kth in range(TOP_K):
        term = g_ref[:, kth:kth + 1] * buf_ref[kth * tm:(kth + 1) * tm, :]
        acc = term if acc is None else acc + term
    o_ref[...] = acc


def moe_combine(ys, e_t, r_t, bstart, gates):
    n_slots, d = ys.shape
    n_tiles, _, per_tile = e_t.shape
    tm = per_tile // TOP_K
    smem = pl.BlockSpec((1, 1, per_tile), lambda i, bs: (i, 0, 0), memory_space=pltpu.SMEM)
    return pl.pallas_call(
        _combine_kernel,
        grid_spec=pltpu.PrefetchScalarGridSpec(
            num_scalar_prefetch=1, grid=(n_tiles,),
            in_specs=[smem, smem, pl.BlockSpec((tm, LANES), lambda i, bs: (i, 0)), pl.BlockSpec(memory_space=pl.ANY)],
            out_specs=pl.BlockSpec((tm, d), lambda i, bs: (i, 0)),
            scratch_shapes=[pltpu.VMEM((TOP_K * tm, d), F32), pltpu.SemaphoreType.DMA(())]),
        out_shape=jax.ShapeDtypeStruct((n_tiles * tm, d), F32),
        compiler_params=_cparams("arbitrary"),
        name="moe_combine",
    )(bstart, e_t, r_t, gates, ys)


def moe_items(nb_e, bstart, bend, n_blocks, nt):
    i = jnp.arange(n_blocks * nt, dtype=jnp.int32)
    used = bend[-1]
    e_i = jnp.minimum(jnp.searchsorted(bend * nt, i, side="right").astype(jnp.int32), N_EXPERTS - 1)
    local = i - bstart[e_i] * nt
    nbe = jnp.maximum(nb_e[e_i], 1)
    valid = i < used * nt
    spare = i - used * nt
    blk = jnp.where(valid, bstart[e_i] + local % nbe, used + spare // nt)
    n_out = jnp.where(valid, local // nbe, spare % nt)
    e_last = jnp.max(jnp.where(nb_e > 0, jnp.arange(N_EXPERTS, dtype=jnp.int32), 0))
    e_w = jnp.where(valid, e_i, e_last)
    n_w = jnp.where(valid, local // nbe, nt - 1)
    first = (valid & (local % nbe == 0)).astype(jnp.int32)
    return blk.astype(jnp.int32), e_w.astype(jnp.int32), n_w.astype(jnp.int32), n_out.astype(jnp.int32), first, \
        valid.astype(jnp.int32)


def _gmm_kernel(blk_ref, ew_ref, nw_ref, no_ref, first_ref, valid_ref, x_ref, *rest, glu):
    i = pl.program_id(0)
    if glu:
        wg_ref, wu_ref, bg_ref, bu_ref, o_ref, wgb_ref, wub_ref = rest
    else:
        w_ref, b_ref, o_ref, wb_ref = rest

    @pl.when(first_ref[i] == 1)
    def _():
        if glu:
            wgb_ref[...] = wg_ref[...].astype(BF16)
            wub_ref[...] = wu_ref[...].astype(BF16)
        else:
            wb_ref[...] = w_ref[...].astype(BF16)

    @pl.when(valid_ref[i] == 1)
    def _():
        x = x_ref[...]
        if glu:
            gate = jnp.dot(x, wgb_ref[...], preferred_element_type=F32) + bg_ref[...]
            up = jnp.dot(x, wub_ref[...], preferred_element_type=F32) + bu_ref[...]
            gate = jnp.minimum(gate, SWIGLU_LIMIT)
            up = jnp.clip(up, -SWIGLU_LIMIT, SWIGLU_LIMIT)
            out = (up + 1.0) * (gate * _sigmoid(SWIGLU_ALPHA * gate))
        else:
            out = jnp.dot(x, wb_ref[...], preferred_element_type=F32) + b_ref[...]
        o_ref[...] = out.astype(o_ref.dtype)

    @pl.when(valid_ref[i] == 0)
    def _():
        o_ref[...] = jnp.zeros(o_ref.shape, o_ref.dtype)


def grouped_matmul(xs, w, b, layer, items, *, glu):
    n_slots, k = xs.shape
    n_out = w.shape[-1] // 2 if glu else w.shape[-1]
    tn = MOE_TN_UP if glu else MOE_TN_DOWN
    n_items = items[0].shape[0]
    b4 = b.reshape(b.shape[0], b.shape[1], 1, b.shape[2])
    wspec = lambda off: pl.BlockSpec((None, None, k, tn),
                                     lambda i, blk, ew, nw, no, fi, va: (layer, ew[i], 0, nw[i] + off))
    bspec = lambda off: pl.BlockSpec((None, None, 1, tn),
                                     lambda i, blk, ew, nw, no, fi, va: (layer, ew[i], 0, nw[i] + off))
    xspec = pl.BlockSpec((MOE_ROWS, k), lambda i, blk, ew, nw, no, fi, va: (blk[i], 0))
    ospec = pl.BlockSpec((MOE_ROWS, tn), lambda i, blk, ew, nw, no, fi, va: (blk[i], no[i]))
    if glu:
        half = n_out // tn
        in_specs = [xspec, wspec(0), wspec(half), bspec(0), bspec(half)]
        args = (xs, w, w, b4, b4)
        scratch = [pltpu.VMEM((k, tn), BF16)] * 2
    else:
        in_specs = [xspec, wspec(0), bspec(0)]
        args = (xs, w, b4)
        scratch = [pltpu.VMEM((k, tn), BF16)]
    return pl.pallas_call(
        functools.partial(_gmm_kernel, glu=glu),
        grid_spec=pltpu.PrefetchScalarGridSpec(
            num_scalar_prefetch=6, grid=(n_items,), in_specs=in_specs, out_specs=ospec, scratch_shapes=scratch),
        out_shape=jax.ShapeDtypeStruct((n_slots, n_out), BF16 if glu else F32),
        compiler_params=_cparams("arbitrary"),
        name="moe_up" if glu else "moe_down",
    )(*items, *args)


def moe_ffn(hn, layer, w_router, b_router, w_gu, b_gu, w_down, b_down):
    n_tok, d = hn.shape
    top_e, gates, rank, counts = moe_router(hn, w_router, b_router, layer)
    nb_e, bstart, bend, n_blocks = moe_blocks(counts[0, :N_EXPERTS], n_tok * TOP_K)
    tm = _pick_tile(n_tok, MOE_TOKENS_PER_STEP)
    per_tile = lambda a: a[:, :TOP_K].reshape(n_tok // tm, 1, tm * TOP_K)
    e_t, r_t = per_tile(top_e), per_tile(rank)
    x_rows = lax.bitcast_convert_type(hn.astype(BF16).reshape(n_tok, d // 2, 2), jnp.uint32)
    xs_rows = moe_dispatch(x_rows, e_t, r_t, bstart, n_blocks * MOE_ROWS)
    xs = lax.bitcast_convert_type(xs_rows, BF16).reshape(n_blocks * MOE_ROWS, d)
    hmid = grouped_matmul(xs, w_gu, b_gu, layer, moe_items(nb_e, bstart, bend, n_blocks, D_FF // MOE_TN_UP), glu=True)
    ys = grouped_matmul(hmid, w_down, b_down, layer,
                        moe_items(nb_e, bstart, bend, n_blocks, D_MODEL // MOE_TN_DOWN), glu=False)
    return moe_combine(ys, e_t, r_t, bstart, gates)


def ssd_mixer(hn, conv_buf, s0, w_in, conv_w, conv_b, dt_bias, a_log, d_skip, norm_g, w_out):
    bsz, seqlen, d = hn.shape
    h2 = hn.reshape(bsz * seqlen, d)
    zxbc = matmul(h2, w_in, w_index=(0,), n_cols=SSD_D_INNER + SSD_CONV_DIM).reshape(bsz, seqlen, -1)
    dt_raw = matmul(h2, ssd_dt_weight(w_in[0])).reshape(bsz, seqlen, -1)
    y, new_buf, s_fin = ssd_core(zxbc, dt_raw, conv_buf, s0, conv_w[0], conv_b[0], dt_bias[0], a_log[0], d_skip[0],
                                 norm_g[0])
    return matmul(y, w_out, w_index=(0,)), new_buf, s_fin


def lru_mixer(hn, conv_buf, h0, w_in, conv_w, conv_b, w_rg, b_rg, w_ig, b_ig, lam, w_out):
    bsz, seqlen, d = hn.shape
    xg = matmul(hn.reshape(bsz * seqlen, d), w_in, w_index=(0,)).reshape(bsz, seqlen, -1)
    y, new_buf, h_fin = lru_core(xg, conv_buf, h0, conv_w[0], conv_b[0], w_rg[0], b_rg[0].reshape(-1), w_ig[0],
                                 b_ig[0].reshape(-1), lam[0])
    return matmul(y, w_out, w_index=(0,)), new_buf, h_fin


def gdn_mixer(hn, conv_buf, s0, w_in, conv_w, a_log, dt_bias, norm_g, w_out):
    bsz, seqlen, d = hn.shape
    h2 = hn.reshape(bsz * seqlen, d)
    qkvz = matmul(h2, w_in, w_index=(0,), n_cols=GDN_CONV_DIM + GDN_VAL_DIM).reshape(bsz, seqlen, -1)
    ba = matmul(h2, gdn_ba_weight(w_in[0])).reshape(bsz, seqlen, -1)
    o, new_buf, s_fin = gdn_core(qkvz, ba, conv_buf, s0, conv_w[0], a_log[0], dt_bias[0], norm_g[0])
    return matmul(o, w_out, w_index=(0,)), new_buf, s_fin


def kernel(x_prompt, x_sample, c_prompt, c_sample, state_ssd, cache_ssd_conv, state_rwkv, cache_rwkv_shift, state_lru, cache_lru_conv, state_gdn, cache_gdn_conv, w_ada, b_ada, norm_g, final_g, ssd_w_in, ssd_conv_w, ssd_conv_b, ssd_dt_bias, ssd_a_log, ssd_d, ssd_norm_g, ssd_w_out, rwkv_mu, rwkv_w_rkv, rwkv_w0, rwkv_w1, rwkv_w2, rwkv_a0, rwkv_a1, rwkv_a2, rwkv_g1, rwkv_g2, rwkv_k_k, rwkv_k_a, rwkv_r_k, rwkv_lnx_g, rwkv_lnx_b, rwkv_w_o, lru_w_in, lru_conv_w, lru_conv_b, lru_w_rg, lru_b_rg, lru_w_ig, lru_b_ig, lru_lambda, lru_w_out, gdn_w_in, gdn_conv_w, gdn_a_log, gdn_dt_bias, gdn_norm_g, gdn_w_out, moe_w_router, moe_b_router, moe_w_gu, moe_b_gu, moe_w_down, moe_b_down):
    assert DEPTH == 4 and w_ada.shape[0] == DEPTH
    d = D_MODEL
    xs = [x_prompt, x_sample]
    n_seq = [x.shape[0] for x in xs]
    n_rows = [x.shape[0] * x.shape[1] for x in xs]
    row0 = [0, n_rows[0]]

    cond = jnp.concatenate([c_prompt, c_sample], axis=0)
    pad = (-cond.shape[0]) % (2 * SUBLANES)
    cond = jnp.pad(cond, ((0, pad), (0, 0)))
    mods = []
    for i in range(DEPTH):
        m = matmul(cond, w_ada, b_ada, w_index=(i,), pre_silu=True)
        mods.append([m[:n_seq[0]].reshape(n_seq[0], 1, 6 * d),
                     m[n_seq[0]:n_seq[0] + n_seq[1]].reshape(n_seq[1], 1, 6 * d)])

    def zeros_like_state(s, bsz):
        return jnp.zeros((bsz,) + s.shape[2:], F32)

    st = {
        "ssd": [zeros_like_state(state_ssd, n_seq[0]), state_ssd[0]],
        "ssd_conv": [zeros_like_state(cache_ssd_conv, n_seq[0]), cache_ssd_conv[0]],
        "rwkv": [zeros_like_state(state_rwkv, n_seq[0]), state_rwkv[0]],
        "rwkv_shift": [zeros_like_state(cache_rwkv_shift, n_seq[0]), cache_rwkv_shift[0]],
        "lru": [zeros_like_state(state_lru, n_seq[0]), state_lru[0]],
        "lru_conv": [zeros_like_state(cache_lru_conv, n_seq[0]), cache_lru_conv[0]],
        "gdn": [zeros_like_state(state_gdn, n_seq[0]), state_gdn[0]],
        "gdn_conv": [zeros_like_state(cache_gdn_conv, n_seq[0]), cache_gdn_conv[0]],
    }
    new = {}

    hn = [norm_mod(xs[gi], norm_g[0, 0], mods[0][gi], 0) for gi in range(2)]
    outs = None
    for i in range(DEPTH):
        ys = []
        for gi in range(2):
            if i == 0:
                y, buf, s = ssd_mixer(hn[gi], st["ssd_conv"][gi], st["ssd"][gi], ssd_w_in, ssd_conv_w, ssd_conv_b,
                                      ssd_dt_bias, ssd_a_log, ssd_d, ssd_norm_g, ssd_w_out)
                new.setdefault("ssd", []).append(s)
                new.setdefault("ssd_conv", []).append(buf)
            elif i == 1:
                y, buf, s = rwkv_mixer(hn[gi], st["rwkv_shift"][gi], st["rwkv"][gi], rwkv_mu[0], rwkv_w_rkv[0],
                                       rwkv_w0[0], rwkv_w1[0], rwkv_w2[0], rwkv_a0[0], rwkv_a1[0], rwkv_a2[0],
                                       rwkv_g1[0], rwkv_g2[0], rwkv_k_k[0], rwkv_k_a[0], rwkv_r_k[0],
                                       rwkv_lnx_g[0], rwkv_lnx_b[0], rwkv_w_o[0])
                new.setdefault("rwkv", []).append(s)
                new.setdefault("rwkv_shift", []).append(buf)
            elif i == 2:
                y, buf, s = lru_mixer(hn[gi], st["lru_conv"][gi], st["lru"][gi], lru_w_in, lru_conv_w, lru_conv_b,
                                      lru_w_rg, lru_b_rg, lru_w_ig, lru_b_ig, lru_lambda, lru_w_out)
                new.setdefault("lru", []).append(s)
                new.setdefault("lru_conv", []).append(buf)
            else:
                y, buf, s = gdn_mixer(hn[gi], st["gdn_conv"][gi], st["gdn"][gi], gdn_w_in, gdn_conv_w, gdn_a_log,
                                      gdn_dt_bias, gdn_norm_g, gdn_w_out)
                new.setdefault("gdn", []).append(s)
                new.setdefault("gdn_conv", []).append(buf)
            ys.append(y)
        hn2 = []
        for gi in range(2):
            xs[gi], h2 = resid_norm(xs[gi], ys[gi], 0, norm_g[i, 1], mods[i][gi], 0, mods[i][gi], 1)
            hn2.append(h2.reshape(n_rows[gi], d))
        moe = moe_ffn(jnp.concatenate(hn2, axis=0), i, moe_w_router, moe_b_router, moe_w_gu, moe_b_gu, moe_w_down,
                      moe_b_down)
        if i + 1 < DEPTH:
            for gi in range(2):
                xs[gi], hn[gi] = resid_norm(xs[gi], moe, row0[gi], norm_g[i + 1, 0], mods[i][gi], 1,
                                            mods[i + 1][gi], 0)
        else:
            outs = [resid_final(xs[gi], moe, row0[gi], final_g, mods[i][gi], 1) for gi in range(2)]

    res = [outs[0], outs[1]]
    for key in ("ssd", "ssd_conv", "rwkv", "rwkv_shift", "lru", "lru_conv", "gdn", "gdn_conv"):
        res += [new[key][0][None], new[key][1][None]]
    return tuple(res)
```

```python
import functools
import math

import jax
import jax.numpy as jnp
from jax import lax
from jax.experimental import pallas as pl
from jax.experimental.pallas import tpu as pltpu

F32 = jnp.float32
BF16 = jnp.bfloat16

VMEM_LIMIT_BYTES = 56 * 1024 * 1024
LANES = 128
SUBLANES = 8

D_MODEL = 2048
DEPTH = 4
CONV_W = 4
NORM_EPS = 1e-6
L2_EPS = 1e-6

LRU_BLOCKS = 8
LRU_C = 8.0


def _cparams(*sem):
    return pltpu.CompilerParams(dimension_semantics=sem, vmem_limit_bytes=VMEM_LIMIT_BYTES)


def _sigmoid(x):
    return 1.0 / (1.0 + jnp.exp(-x))


def _silu(x):
    return x * _sigmoid(x)


def _softplus(x):
    return jnp.maximum(x, 0.0) + jnp.log(1.0 + jnp.exp(-jnp.abs(x)))


def _expm1(x):
    u = jnp.exp(x)
    safe = jnp.where(u == 1.0, 2.0, u)
    return jnp.where(u == 1.0, x, (u - 1.0) * x / jnp.log(safe))


def _dims(a, ca, cb):
    lead = a.ndim - 2
    batch = tuple(range(lead))
    return (((ca + lead,), (cb + lead,)), (batch, batch))


def _dot(a, b):
    return lax.dot_general(a.astype(BF16), b.astype(BF16), _dims(a, 1, 0), preferred_element_type=F32)


def _dot_nt(a, b):
    return lax.dot_general(a.astype(BF16), b.astype(BF16), _dims(a, 1, 1), preferred_element_type=F32)


def _dot_tn(a, b):
    return lax.dot_general(a.astype(BF16), b.astype(BF16), _dims(a, 0, 0), preferred_element_type=F32)


def _split3(a):
    hi = a.astype(BF16)
    r1 = a - hi.astype(F32)
    mid = r1.astype(BF16)
    lo = (r1 - mid.astype(F32)).astype(BF16)
    return hi, mid, lo


def _dot3(a, b):
    ah, am, _ = _split3(a)
    bh, bm, _ = _split3(b)
    return (_dot(ah, bm) + _dot(am, bh)) + _dot(ah, bh)


def _dot6(a, b):
    ah, am, al = _split3(a)
    bh, bm, bl = _split3(b)
    dot = lambda x, y: jnp.dot(x, y, preferred_element_type=F32)
    small = dot(am, bm) + dot(ah, bl) + dot(al, bh)
    return (small + dot(ah, bm) + dot(am, bh)) + dot(ah, bh)


def _mm_kernel(x_ref, w_ref, *rest, has_bias, pre_silu):
    if has_bias:
        b_ref, o_ref, xb_ref = rest
    else:
        o_ref, xb_ref = rest

    @pl.when(pl.program_id(1) == 0)
    def _():
        x = x_ref[...]
        xb_ref[...] = (_silu(x) if pre_silu else x).astype(BF16)

    acc = jnp.dot(xb_ref[...], w_ref[...].astype(BF16), preferred_element_type=F32)
    if has_bias:
        acc = acc + b_ref[...]
    o_ref[...] = acc.astype(o_ref.dtype)


def _pick_tile(n, pref):
    t = min(n, pref)
    while n % t:
        t //= 2
    return t


def matmul(x, w, b=None, *, w_index=(), n_cols=None, col0=0, out_dtype=F32, tm=None, tn=None, pre_silu=False):
    m, k = x.shape
    lead = len(w_index)
    assert w.shape[lead] == k
    n_total = w.shape[lead + 1]
    n = n_total if n_cols is None else n_cols
    if tm is None:
        tm = _pick_tile(m, 1024 if k <= 2048 else 512)
    if tn is None:
        tn = _pick_tile(n, 512)
    assert m % tm == 0 and n % tn == 0 and col0 % tn == 0
    cb = col0 // tn
    widx = tuple(w_index)
    in_specs = [
        pl.BlockSpec((tm, k), lambda i, j: (i, 0)),
        pl.BlockSpec((None,) * lead + (k, tn), lambda i, j: widx + (0, j + cb)),
    ]
    args = [x, w]
    if b is not None:
        bb = b.reshape(b.shape[:lead] + (1, n_total))
        in_specs.append(pl.BlockSpec((None,) * lead + (1, tn), lambda i, j: widx + (0, j + cb)))
        args.append(bb)
    return pl.pallas_call(
        functools.partial(_mm_kernel, has_bias=b is not None, pre_silu=pre_silu),
        grid=(m // tm, n // tn),
        in_specs=in_specs,
        out_specs=pl.BlockSpec((tm, tn), lambda i, j: (i, j)),
        out_shape=jax.ShapeDtypeStruct((m, n), out_dtype),
        scratch_shapes=[pltpu.VMEM((tm, k), BF16)],
        compiler_params=_cparams("parallel", "arbitrary"),
        name="matmul",
    )(*args)


def _rms(x, g):
    return x * lax.rsqrt(jnp.mean(x * x, axis=-1, keepdims=True) + NORM_EPS) * g


def _norm_mod_kernel(x_ref, g_ref, sh_ref, sc_ref, o_ref):
    y = _rms(x_ref[...], g_ref[...])
    o_ref[...] = y * (1.0 + sc_ref[...]) + sh_ref[...]


def _seq_blocks(bsz, seqlen):
    if seqlen >= 256:
        return 1, 256
    return max(1, 128 // seqlen), seqlen


def _mod_spec(bb, col):
    return pl.BlockSpec((bb, 1, D_MODEL), lambda i, j: (i, 0, col))


def norm_mod(x, g, mod, which):
    bsz, seqlen, d = x.shape
    bb, bl = _seq_blocks(bsz, seqlen)
    xspec = pl.BlockSpec((bb, bl, d), lambda i, j: (i, j, 0))
    return pl.pallas_call(
        _norm_mod_kernel,
        grid=(bsz // bb, seqlen // bl),
        in_specs=[xspec, pl.BlockSpec((1, 1, d), lambda i, j: (0, 0, 0)),
                  _mod_spec(bb, 3 * which), _mod_spec(bb, 3 * which + 1)],
        out_specs=xspec,
        out_shape=jax.ShapeDtypeStruct(x.shape, F32),
        compiler_params=_cparams("parallel", "parallel"),
        name="norm_mod",
    )(x, g.reshape(1, 1, d), mod, mod)


def _resid_norm_kernel(x_ref, y_ref, gt_ref, g_ref, sh_ref, sc_ref, xo_ref, ho_ref):
    xn = x_ref[...] + gt_ref[...] * y_ref[...]
    xo_ref[...] = xn
    ho_ref[...] = _rms(xn, g_ref[...]) * (1.0 + sc_ref[...]) + sh_ref[...]


def _rows_view(y, row0, bsz, seqlen, bb, bl):
    d = y.shape[-1]
    assert bb == 1 or bl == seqlen
    assert row0 % (bb * bl) == 0
    base, per_seq = row0 // (bb * bl), seqlen // bl
    return y.reshape(-1, bl, d), pl.BlockSpec((bb, bl, d), lambda i, j: (base + i * per_seq + j, 0, 0))


def resid_norm(x, y, row0, g, mod_gate, which_gate, mod_norm, which_norm):
    bsz, seqlen, d = x.shape
    bb, bl = _seq_blocks(bsz, seqlen)
    xspec = pl.BlockSpec((bb, bl, d), lambda i, j: (i, j, 0))
    y3, yspec = _rows_view(y, row0, bsz, seqlen, bb, bl)
    return pl.pallas_call(
        _resid_norm_kernel,
        grid=(bsz // bb, seqlen // bl),
        in_specs=[xspec, yspec, _mod_spec(bb, 3 * which_gate + 2),
                  pl.BlockSpec((1, 1, d), lambda i, j: (0, 0, 0)),
                  _mod_spec(bb, 3 * which_norm), _mod_spec(bb, 3 * which_norm + 1)],
        out_specs=[xspec, xspec],
        out_shape=[jax.ShapeDtypeStruct(x.shape, F32)] * 2,
        compiler_params=_cparams("parallel", "parallel"),
        name="resid_norm",
    )(x, y3, mod_gate, g.reshape(1, 1, d), mod_norm, mod_norm)


def _resid_final_kernel(x_ref, y_ref, gt_ref, g_ref, o_ref):
    o_ref[...] = _rms(x_ref[...] + gt_ref[...] * y_ref[...], g_ref[...])


def resid_final(x, y, row0, g, mod_gate, which_gate):
    bsz, seqlen, d = x.shape
    bb, bl = _seq_blocks(bsz, seqlen)
    xspec = pl.BlockSpec((bb, bl, d), lambda i, j: (i, j, 0))
    y3, yspec = _rows_view(y, row0, bsz, seqlen, bb, bl)
    return pl.pallas_call(
        _resid_final_kernel,
        grid=(bsz // bb, seqlen // bl),
        in_specs=[xspec, yspec, _mod_spec(bb, 3 * which_gate + 2),
                  pl.BlockSpec((1, 1, d), lambda i, j: (0, 0, 0))],
        out_specs=xspec,
        out_shape=jax.ShapeDtypeStruct(x.shape, F32),
        compiler_params=_cparams("parallel", "parallel"),
        name="resid_final",
    )(x, y3, mod_gate, g.reshape(1, 1, d))


def _shift_time(x, d, fill, axis):
    t = lax.broadcasted_iota(jnp.int32, x.shape, axis)
    return jnp.where(t >= d, pltpu.roll(x, d, axis), fill)


def _conv_silu_free(xp_ref, w_ref, b, tlen):
    acc = None
    for k in range(CONV_W):
        term = xp_ref[:, pl.ds(SUBLANES - (CONV_W - 1) + k, tlen), :] * w_ref[k:k + 1, :]
        acc = term if acc is None else acc + term
    if b is not None:
        acc = acc + b
    return acc


def _load_history(xp_ref, x_ref, buf_ref, tlen):
    j = pl.program_id(1)

    @pl.when(j == 0)
    def _():
        xp_ref[:, SUBLANES - (CONV_W - 1):SUBLANES, :] = buf_ref[...]

    @pl.when(j > 0)
    def _():
        xp_ref[:, SUBLANES - (CONV_W - 1):SUBLANES, :] = xp_ref[:, tlen + SUBLANES - (CONV_W - 1):tlen + SUBLANES, :]

    xp_ref[:, SUBLANES:SUBLANES + tlen, :] = x_ref[...]


def _gelu_tanh(x):
    return 0.5 * x * (1.0 + jnp.tanh(math.sqrt(2.0 / math.pi) * (x + 0.044715 * (x * x * x))))


def _lru_kernel(xb_ref, gb_ref, buf_ref, h0_ref, cw_ref, cb_ref, wrg_ref, brg_ref, wig_ref, big_ref, lam_ref,
                y_ref, nbuf_ref, hfin_ref, xp_ref, h_ref, *, tlen):
    j = pl.program_id(1)
    bb = xb_ref.shape[0]
    width = xb_ref.shape[2]
    blk = width // LRU_BLOCKS

    @pl.when(j == 0)
    def _():
        h_ref[...] = h0_ref[...]

    _load_history(xp_ref, xb_ref, buf_ref, tlen)
    xc = _conv_silu_free(xp_ref, cw_ref, cb_ref[...], tlen)
    x2 = xc.reshape(bb * tlen, width)
    gr, gi = [], []
    for h in range(LRU_BLOCKS):
        xh = x2[:, h * blk:(h + 1) * blk]
        gr.append(_dot(xh, wrg_ref[h]))
        gi.append(_dot(xh, wig_ref[h]))
    gate_r = _sigmoid(jnp.concatenate(gr, axis=-1) + brg_ref[...]).reshape(bb, tlen, width)
    gate_i = _sigmoid(jnp.concatenate(gi, axis=-1) + big_ref[...]).reshape(bb, tlen, width)
    log_a = (-LRU_C) * gate_r * _softplus(-lam_ref[...])
    a = jnp.exp(log_a)
    bx = jnp.sqrt(-_expm1(2.0 * log_a)) * gate_i * xc
    d = 1
    while d < tlen:
        bx = a * _shift_time(bx, d, 0.0, 1) + bx
        a = a * _shift_time(a, d, 1.0, 1)
        d *= 2
    hs = bx + a * h_ref[...]
    h_ref[...] = hs[:, tlen - 1:tlen, :]
    y = hs * _gelu_tanh(gb_ref[...])
    y_ref[...] = y.reshape(bb * tlen, width).astype(y_ref.dtype)

    @pl.when(j == pl.num_programs(1) - 1)
    def _():
        nbuf_ref[...] = xp_ref[:, tlen + SUBLANES - (CONV_W - 1):tlen + SUBLANES, :]
        hfin_ref[...] = hs[:, tlen - 1:tlen, :]


def lru_core(xg, conv_buf, h0, conv_w, conv_b, w_rg, b_rg, w_ig, b_ig, lam):
    bsz, seqlen, w2 = xg.shape
    width = w2 // 2
    bb, tlen = _seq_blocks(bsz, seqlen)
    nj = seqlen // tlen
    row = lambda v: v.reshape(1, 1, width)
    full3 = pl.BlockSpec((1, 1, width), lambda i, j: (0, 0, 0))
    wspec = pl.BlockSpec(w_rg.shape, lambda i, j: (0, 0, 0))
    y, nbuf, hfin = pl.pallas_call(
        functools.partial(_lru_kernel, tlen=tlen),
        grid=(bsz // bb, nj),
        in_specs=[pl.BlockSpec((bb, tlen, width), lambda i, j: (i, j, 0)),
                  pl.BlockSpec((bb, tlen, width), lambda i, j: (i, j, 1)),
                  pl.BlockSpec((bb, CONV_W - 1, width), lambda i, j: (i, 0, 0)),
                  pl.BlockSpec((bb, 1, width), lambda i, j: (i, 0, 0)),
                  pl.BlockSpec((CONV_W, width), lambda i, j: (0, 0)),
                  full3, wspec, full3, wspec, full3, full3],
        out_specs=[pl.BlockSpec((bb * tlen, width), lambda i, j: (i * nj + j, 0)),
                   pl.BlockSpec((bb, CONV_W - 1, width), lambda i, j: (i, 0, 0)),
                   pl.BlockSpec((bb, 1, width), lambda i, j: (i, 0, 0))],
        out_shape=[jax.ShapeDtypeStruct((bsz * seqlen, width), BF16),
                   jax.ShapeDtypeStruct((bsz, CONV_W - 1, width), F32),
                   jax.ShapeDtypeStruct((bsz, 1, width), F32)],
        scratch_shapes=[pltpu.VMEM((bb, tlen + SUBLANES, width), F32), pltpu.VMEM((bb, 1, width), F32)],
        compiler_params=_cparams("parallel", "arbitrary"),
        name="lru_core",
    )(xg, xg, conv_buf, h0.reshape(bsz, 1, width), conv_w, row(conv_b), w_rg, row(b_rg), w_ig, row(b_ig), row(lam))
    return y, nbuf, hfin.reshape(bsz, width)


SSD_HEADDIM = 64
SSD_GROUPS = 8
SSD_HPG = 8
SSD_HEADS = SSD_GROUPS * SSD_HPG
SSD_STATE = 128
SSD_D_INNER = SSD_HEADS * SSD_HEADDIM
SSD_GROUP_W = SSD_HPG * SSD_HEADDIM
SSD_CONV_DIM = SSD_D_INNER + 2 * SSD_GROUPS * SSD_STATE
SSD_QP = 128
SSD_GROUPS_PER_STEP = 4
SSD_GROUPS_PER_STEP_SHORT = 8


def _cumsum_time(x):
    n = x.shape[0]
    d = 1
    while d < n:
        x = x + _shift_time(x, d, 0.0, 0)
        d *= 2
    return x


def _pad_rows(x, rows):
    if x.shape[0] == rows:
        return x
    return jnp.concatenate([x, jnp.zeros((rows - x.shape[0],) + x.shape[1:], x.dtype)], axis=0)


def _lane_half(shape):
    return lax.broadcasted_iota(jnp.int32, shape, 1) < SSD_HEADDIM


def _ssd_kernel(z_ref, x_ref, b_ref, c_ref, dt_ref, bufx_ref, bufb_ref, bufc_ref, s0_ref,
                cwx_ref, cwb_ref, cwc_ref, cbx_ref, cbb_ref, cbc_ref, dtb_ref, alog_ref, dsk_ref, ng_ref,
                y_ref, nbx_ref, nbb_ref, nbc_ref, s_ref, xpx_ref, xpb_ref, xpc_ref, *, q):
    j = pl.program_id(2)
    qp = max(q, SUBLANES)

    @pl.when(j == 0)
    def _():
        s_ref[...] = s0_ref[...]

    def conv(xp_ref, x_in, buf, cw, cb, nb_ref):
        @pl.when(j == 0)
        def _():
            xp_ref[:, SUBLANES - (CONV_W - 1):SUBLANES, :] = buf[...]

        @pl.when(j > 0)
        def _():
            xp_ref[:, SUBLANES - (CONV_W - 1):SUBLANES, :] = xp_ref[:, q + SUBLANES - (CONV_W - 1):q + SUBLANES, :]

        xp_ref[:, SUBLANES:SUBLANES + q, :] = x_in[...]

        @pl.when(j == pl.num_programs(2) - 1)
        def _():
            nb_ref[...] = xp_ref[:, q + SUBLANES - (CONV_W - 1):q + SUBLANES, :]

        return _pad_rows(_silu(_conv_silu_free(xp_ref, cw, cb[...], q)[0]), qp)

    ngrp = x_ref.shape[2] // SSD_GROUP_W
    xs = conv(xpx_ref, x_ref, bufx_ref, cwx_ref, cbx_ref, nbx_ref)
    bm = conv(xpb_ref, b_ref, bufb_ref, cwb_ref, cbb_ref, nbb_ref)
    cm = conv(xpc_ref, c_ref, bufc_ref, cwc_ref, cbc_ref, nbc_ref)

    row = lax.broadcasted_iota(jnp.int32, (qp, LANES), 0)
    causal = lax.broadcasted_iota(jnp.int32, (qp, qp), 0) >= lax.broadcasted_iota(jnp.int32, (qp, qp), 1)
    lane_lo = _lane_half((qp, LANES))
    row_lo = lax.broadcasted_iota(jnp.int32, (LANES, SSD_STATE), 0) < SSD_HEADDIM
    npairs = SSD_HPG // 2
    lanes = lambda t, i: t[:, i * LANES:(i + 1) * LANES]

    def col(v, h, n=LANES):
        return jnp.broadcast_to(v[:, h:h + 1], (qp, n))

    bm3 = jnp.stack([lanes(bm, g) for g in range(ngrp)])
    cm3 = jnp.stack([lanes(cm, g) for g in range(ngrp)])
    cb3 = _dot_nt(cm3, bm3)
    m_head, x_pair, ecum_p, toend_p, el_p = [], [], [], [], []
    for g in range(ngrp):
        dt = jnp.where(row < q, _softplus(_pad_rows(lanes(dt_ref[0], g), qp) + dtb_ref[g]), 0.0)
        cum = _cumsum_time(dt * (-jnp.exp(alog_ref[g])))
        last = cum[qp - 1:qp, :]
        ecum, toend, elast = jnp.exp(cum), jnp.exp(last - cum) * dt, jnp.exp(last)
        cum_t = _transpose_tile(cum)
        dt_t = _transpose_tile(dt)
        for h in range(SSD_HPG):
            seg = col(cum, h, qp) - cum_t[h:h + 1, :]
            m_head.append(cb3[g] * jnp.exp(jnp.where(causal, seg, -jnp.inf)) * dt_t[h:h + 1, :])
        for p in range(npairs):
            x_pair.append(lanes(xs, g * npairs + p))
            ecum_p.append(jnp.where(lane_lo, col(ecum, 2 * p), col(ecum, 2 * p + 1)))
            toend_p.append(jnp.where(lane_lo, col(toend, 2 * p), col(toend, 2 * p + 1)))
            el_p.append(jnp.where(row_lo, jnp.broadcast_to(elast[:, 2 * p:2 * p + 1], (LANES, SSD_STATE)),
                                  jnp.broadcast_to(elast[:, 2 * p + 1:2 * p + 2], (LANES, SSD_STATE))))
    nh, npr = ngrp * SSD_HPG, ngrp * npairs
    x_pair = jnp.stack(x_pair)
    intra = _dot(jnp.stack(m_head), jnp.stack([x_pair[h // 2] for h in range(nh)]))
    s_pair = s_ref[0].reshape(npr, 2 * SSD_HEADDIM, SSD_STATE)
    cm_b = jnp.stack([cm3[i // npairs] for i in range(npr)])
    bm_b = jnp.stack([bm3[i // npairs] for i in range(npr)])
    y = jnp.stack([jnp.where(lane_lo, intra[2 * i], intra[2 * i + 1]) for i in range(npr)])
    y = y + _dot_nt(cm_b, s_pair) * jnp.stack(ecum_p)
    s_new = s_pair * jnp.stack(el_p) + _dot_tn(x_pair * jnp.stack(toend_p), bm_b)
    s_ref[0] = s_new.reshape(nh, SSD_HEADDIM, SSD_STATE)
    y = jnp.concatenate([y[i] for i in range(npr)], axis=-1) + dsk_ref[...] * xs

    yz = y[:q] * _silu(z_ref[0])
    for g in range(ngrp):
        gsl = slice(g * SSD_GROUP_W, (g + 1) * SSD_GROUP_W)
        yg = yz[:, gsl]
        yg = yg * lax.rsqrt(jnp.mean(yg * yg, axis=-1, keepdims=True) + NORM_EPS) * ng_ref[:, gsl]
        y_ref[:, gsl] = yg.astype(y_ref.dtype)


def ssd_core(zxbc, dt_raw, conv_buf, s0, conv_w, conv_b, dt_bias, a_log, d_skip, norm_g):
    bsz, seqlen, _ = zxbc.shape
    q = min(seqlen, SSD_QP)
    nc = seqlen // q
    g8 = SSD_GROUPS

    def grp_pad(v):
        return jnp.pad(v.reshape(g8, 1, SSD_HPG), ((0, 0), (0, 0), (0, LANES - SSD_HPG)))

    d_exp = jnp.repeat(d_skip, SSD_HEADDIM).reshape(1, SSD_D_INNER)
    cb2 = conv_b.reshape(1, SSD_CONV_DIM)
    ngrp = SSD_GROUPS_PER_STEP if nc > 1 else SSD_GROUPS_PER_STEP_SHORT
    xw, sw = ngrp * SSD_GROUP_W, ngrp * SSD_STATE
    xoff = SSD_D_INNER // xw
    boff = 2 * SSD_D_INNER // sw
    coff = boff + SSD_GROUPS // ngrp
    cboff = SSD_D_INNER // sw
    ccoff = cboff + SSD_GROUPS // ngrp
    seq3 = lambda w, off: pl.BlockSpec((1, q, w), lambda b, g, j: (b, j, g + off))
    buf3 = lambda w, off: pl.BlockSpec((1, CONV_W - 1, w), lambda b, g, j: (b, 0, g + off))
    par2 = lambda r, w, off: pl.BlockSpec((r, w), lambda b, g, j: (0, g + off))
    grp = pl.BlockSpec((ngrp, 1, LANES), lambda b, g, j: (g, 0, 0))
    sspec = pl.BlockSpec((1, ngrp * SSD_HPG, SSD_HEADDIM, SSD_STATE), lambda b, g, j: (b, g, 0, 0))
    y, nbx, nbb, nbc, sfin = pl.pallas_call(
        functools.partial(_ssd_kernel, q=q),
        grid=(bsz, g8 // ngrp, nc),
        in_specs=[seq3(xw, 0), seq3(xw, xoff), seq3(sw, boff), seq3(sw, coff),
                  seq3(sw, 0),
                  buf3(xw, 0), buf3(sw, cboff), buf3(sw, ccoff), sspec,
                  par2(CONV_W, xw, 0), par2(CONV_W, sw, cboff), par2(CONV_W, sw, ccoff),
                  par2(1, xw, 0), par2(1, sw, cboff), par2(1, sw, ccoff),
                  grp, grp, par2(1, xw, 0), par2(1, xw, 0)],
        out_specs=[pl.BlockSpec((q, xw), lambda b, g, j: (b * nc + j, g)),
                   buf3(xw, 0), buf3(sw, 0), buf3(sw, 0), sspec],
        out_shape=[jax.ShapeDtypeStruct((bsz * seqlen, SSD_D_INNER), F32),
                   jax.ShapeDtypeStruct((bsz, CONV_W - 1, SSD_D_INNER), F32),
                   jax.ShapeDtypeStruct((bsz, CONV_W - 1, SSD_GROUPS * SSD_STATE), F32),
                   jax.ShapeDtypeStruct((bsz, CONV_W - 1, SSD_GROUPS * SSD_STATE), F32),
                   jax.ShapeDtypeStruct(s0.shape, F32)],
        scratch_shapes=[pltpu.VMEM((1, q + SUBLANES, xw), F32),
                        pltpu.VMEM((1, q + SUBLANES, sw), F32),
                        pltpu.VMEM((1, q + SUBLANES, sw), F32)],
        compiler_params=_cparams("parallel", "parallel", "arbitrary"),
        name="ssd_core",
    )(zxbc, zxbc, zxbc, zxbc, dt_raw, conv_buf, conv_buf, conv_buf, s0,
      conv_w, conv_w, conv_w, cb2, cb2, cb2, grp_pad(dt_bias), grp_pad(a_log), d_exp, norm_g.reshape(1, SSD_D_INNER))
    return y, jnp.concatenate([nbx, nbb, nbc], axis=-1), sfin


def ssd_dt_weight(w_in):
    w_dt = w_in[:, SSD_D_INNER + SSD_CONV_DIM:]
    w_dt = w_dt.reshape(-1, SSD_GROUPS, SSD_HPG)
    return jnp.pad(w_dt, ((0, 0), (0, 0), (0, LANES - SSD_HPG))).reshape(-1, SSD_GROUPS * LANES)


RWKV_HEAD = 64
RWKV_HEADS = D_MODEL // RWKV_HEAD
RWKV_GN_EPS = 64e-5
RWKV_CHUNK = 64
RWKV_PAIRS_PER_STEP = 8
RWKV_PAIRS_PER_STEP_SHORT = 16


def _rwkv_mix_kernel(x_ref, buf_ref, mu_ref, *rest):
    outs, last_ref, carry_ref = rest[:6], rest[6], rest[7]
    j = pl.program_id(1)
    bb, bl, d = x_ref.shape

    @pl.when(j == 0)
    def _():
        carry_ref[...] = buf_ref[...]

    x = x_ref[...]
    t = lax.broadcasted_iota(jnp.int32, x.shape, 1)
    prev = jnp.where(t >= 1, pltpu.roll(x, 1, 1), carry_ref[...])
    carry_ref[...] = x[:, bl - 1:bl, :]
    diff = prev - x
    for s in range(6):
        outs[s][...] = (x + diff * mu_ref[s:s + 1, :]).reshape(bb * bl, d).astype(BF16)

    @pl.when(j == pl.num_programs(1) - 1)
    def _():
        last_ref[...] = x[:, bl - 1:bl, :]


def rwkv_mix(x, shift_buf, mu):
    bsz, seqlen, d = x.shape
    bb, bl = _seq_blocks(bsz, seqlen)
    nj = seqlen // bl
    ospec = pl.BlockSpec((bb * bl, d), lambda i, j: (i * nj + j, 0))
    cspec = pl.BlockSpec((bb, 1, d), lambda i, j: (i, 0, 0))
    outs = pl.pallas_call(
        _rwkv_mix_kernel,
        grid=(bsz // bb, nj),
        in_specs=[pl.BlockSpec((bb, bl, d), lambda i, j: (i, j, 0)), cspec,
                  pl.BlockSpec((6, d), lambda i, j: (0, 0))],
        out_specs=[ospec] * 6 + [cspec],
        out_shape=[jax.ShapeDtypeStruct((bsz * seqlen, d), BF16)] * 6 + [jax.ShapeDtypeStruct((bsz, 1, d), F32)],
        scratch_shapes=[pltpu.VMEM((bb, 1, d), F32)],
        compiler_params=_cparams("parallel", "arbitrary"),
        name="rwkv_mix",
    )(x, shift_buf.reshape(bsz, 1, d), mu)
    return outs[:6], outs[6].reshape(bsz, d)


def _lora_kernel(x_ref, w1_ref, w2_ref, *rest, act, has_bias):
    if has_bias:
        b_ref, o_ref = rest
    else:
        (o_ref,) = rest
    hmid = _dot(x_ref[...], w1_ref[...])
    if act == "tanh":
        hmid = jnp.tanh(hmid)
    elif act == "sigmoid":
        hmid = _sigmoid(hmid)
    out = _dot(hmid, w2_ref[...])
    if has_bias:
        out = out + b_ref[...]
    o_ref[...] = out


def lora(x, w1, w2, bias, act):
    m, k = x.shape
    r = w1.shape[1]
    n = w2.shape[1]
    tm = _pick_tile(m, 512)
    in_specs = [pl.BlockSpec((tm, k), lambda i: (i, 0)), pl.BlockSpec((k, r), lambda i: (0, 0)),
                pl.BlockSpec((r, n), lambda i: (0, 0))]
    args = [x, w1, w2]
    if bias is not None:
        in_specs.append(pl.BlockSpec((1, n), lambda i: (0, 0)))
        args.append(bias.reshape(1, n))
    return pl.pallas_call(
        functools.partial(_lora_kernel, act=act, has_bias=bias is not None),
        grid=(m // tm,),
        in_specs=in_specs,
        out_specs=pl.BlockSpec((tm, n), lambda i: (i, 0)),
        out_shape=jax.ShapeDtypeStruct((m, n), F32),
        compiler_params=_cparams("parallel"),
        name="lora",
    )(*args)


def _half_sum(x, lo):
    s0 = jnp.sum(jnp.where(lo, x, 0.0), axis=-1, keepdims=True)
    s1 = jnp.sum(jnp.where(lo, 0.0, x), axis=-1, keepdims=True)
    return jnp.where(lo, s0, s1)


def _stack_heads(x, lo):
    return jnp.concatenate([jnp.where(lo, x, 0.0), jnp.where(lo, 0.0, x)], axis=0)


def _rwkv_kernel(r_ref, k_ref, v_ref, w_ref, a_ref, g_ref, s0_ref, kk_ref, ka_ref, rk_ref, lg_ref, lb_ref,
                 y_ref, sfin_ref, st_ref, *, c):
    j = pl.program_id(2)
    hd = RWKV_HEAD
    npair = r_ref.shape[2] // LANES
    lo = lax.broadcasted_iota(jnp.int32, (c, LANES), 1) < hd
    zero = jnp.zeros((hd, hd), F32)
    pl_ = lambda p: slice(p * LANES, (p + 1) * LANES)

    @pl.when(j == 0)
    def _():
        for p in range(npair):
            st_ref[p] = jnp.concatenate([jnp.concatenate([s0_ref[0, 2 * p], zero], axis=1),
                                         jnp.concatenate([zero, s0_ref[0, 2 * p + 1]], axis=1)], axis=0)

    def pairwise(fn, *xs):
        return jnp.concatenate([fn(*[x[:, pl_(p)] for x in xs]) for p in range(npair)], axis=-1)

    half_sum = lambda x: pairwise(lambda t: _half_sum(t, lo), x)
    r, k, v = r_ref[0], k_ref[0], v_ref[0]
    lw = -jnp.exp(-_softplus(-w_ref[0]) - 0.5)
    a = _sigmoid(a_ref[0])
    kkr = k * kk_ref[...]
    kk = kkr * lax.rsqrt(half_sum(kkr * kkr) + L2_EPS)
    k = k * (1.0 + (a - 1.0) * ka_ref[...])
    cum = _cumsum_time(lw)
    gam = jnp.exp(cum)
    inv = jnp.exp(-cum)
    stack = lambda x: jnp.stack([_stack_heads(x[:, pl_(p)], lo) for p in range(npair)])
    glast = jnp.stack([gam[c - 1:c, pl_(p)] for p in range(npair)])
    ka_s = stack(jnp.exp(cum - lw) * kk)
    al_s = stack(kk * a * inv)
    k_s = stack(k * inv)
    r_s = stack(r * gam)
    v_s = stack(v)

    n = 2 * c
    ti = lax.broadcasted_iota(jnp.int32, (1, n, n), 1) % c
    si = lax.broadcasted_iota(jnp.int32, (1, n, n), 2) % c
    strict, incl = ti > si, ti >= si
    st = st_ref[...]
    a_mat = jnp.where(strict, _dot_nt(ka_s, al_s), 0.0)
    b_mat = jnp.where(strict, _dot_nt(ka_s, k_s), 0.0)
    d = _dot(_unit_lower_inverse(a_mat, c), _dot_nt(ka_s, st) + _dot(b_mat, v_s))
    o = (_dot_nt(r_s, st) - _dot(jnp.where(incl, _dot_nt(r_s, al_s), 0.0), d)
         + _dot(jnp.where(incl, _dot_nt(r_s, k_s), 0.0), v_s))
    st_ref[...] = st * glast + _dot_tn(v_s, k_s * glast) - _dot_tn(d, al_s * glast)

    o = o[:, :c] + o[:, c:]
    o = jnp.concatenate([o[p] for p in range(npair)], axis=-1)
    mean = half_sum(o) * (1.0 / hd)
    var = half_sum(jnp.square(o - mean)) * (1.0 / hd)
    o = (o - mean) * lax.rsqrt(var + RWKV_GN_EPS) * lg_ref[...] + lb_ref[...]
    o = o + half_sum(r * k * rk_ref[...]) * v
    y_ref[...] = (o * g_ref[0]).astype(y_ref.dtype)

    @pl.when(j == pl.num_programs(2) - 1)
    def _():
        for p in range(npair):
            fin = st_ref[p]
            sfin_ref[0, 2 * p] = fin[:hd, :hd]
            sfin_ref[0, 2 * p + 1] = fin[hd:, hd:]


def rwkv_core(r, k, v, w_raw, a_pre, g, s0, k_k, k_a, r_k, lnx_g, lnx_b):
    bsz, seqlen, d = r.shape
    c = min(seqlen, RWKV_CHUNK)
    nc = seqlen // c
    npair = RWKV_PAIRS_PER_STEP if nc > 1 else RWKV_PAIRS_PER_STEP_SHORT
    wid = npair * LANES
    seq = pl.BlockSpec((1, c, wid), lambda b, p, j: (b, j, p))
    par = pl.BlockSpec((1, wid), lambda b, p, j: (0, p))
    sspec = pl.BlockSpec((1, 2 * npair, RWKV_HEAD, RWKV_HEAD), lambda b, p, j: (b, p, 0, 0))
    row = lambda t: t.reshape(1, d)
    return pl.pallas_call(
        functools.partial(_rwkv_kernel, c=c),
        grid=(bsz, d // wid, nc),
        in_specs=[seq] * 6 + [sspec] + [par] * 5,
        out_specs=[pl.BlockSpec((c, wid), lambda b, p, j: (b * nc + j, p)), sspec],
        out_shape=[jax.ShapeDtypeStruct((bsz * seqlen, d), F32), jax.ShapeDtypeStruct(s0.shape, F32)],
        scratch_shapes=[pltpu.VMEM((npair, LANES, LANES), F32)],
        compiler_params=_cparams("parallel", "parallel", "arbitrary"),
        name="rwkv_core",
    )(r, k, v, w_raw, a_pre, g, s0, row(k_k), row(k_a), row(r_k), row(lnx_g), row(lnx_b))


def rwkv_mixer(hn, shift_buf, s0, mu, w_rkv, w0, w1, w2, a0, a1, a2, g1, g2, k_k, k_a, r_k, lnx_g, lnx_b, w_o):
    bsz, seqlen, d = hn.shape
    (xr, xk, xv, xw, xa, xg), new_shift = rwkv_mix(hn, shift_buf, mu)
    as3 = lambda t: t.reshape(bsz, seqlen, d)
    r, k, v = (as3(matmul(x, w_rkv, w_index=(s,))) for s, x in enumerate((xr, xk, xv)))
    w_raw = as3(lora(xw, w1, w2, w0, "tanh"))
    a_pre = as3(lora(xa, a1, a2, a0, "none"))
    g = as3(lora(xg, g1, g2, None, "sigmoid"))
    y, s_fin = rwkv_core(r, k, v, w_raw, a_pre, g, s0, k_k, k_a, r_k.reshape(-1), lnx_g, lnx_b)
    return matmul(y, w_o), new_shift, s_fin


GDN_K_HEADS = 16
GDN_V_HEADS = 32
GDN_REP = GDN_V_HEADS // GDN_K_HEADS
GDN_HEAD = 128
GDN_KEY_DIM = GDN_K_HEADS * GDN_HEAD
GDN_VAL_DIM = GDN_V_HEADS * GDN_HEAD
GDN_CONV_DIM = 2 * GDN_KEY_DIM + GDN_VAL_DIM
GDN_CP = 64
GDN_HEADS_PER_STEP = 8
GDN_HEADS_PER_STEP_SHORT = 16


def _transpose_tile(x):
    r = x.shape[0]
    return _pad_rows(x, LANES).T[:, :r]


def _l2norm(x):
    return x * lax.rsqrt(jnp.sum(x * x, axis=-1, keepdims=True) + L2_EPS)


def _unit_lower_inverse(a_strict, order=None):
    n = a_strict.shape[-1]
    order = n if order is None else order
    eye = (lax.broadcasted_iota(jnp.int32, (n, n), 0) == lax.broadcasted_iota(jnp.int32, (n, n), 1)).astype(F32)
    p = -a_strict
    t = eye + p
    d = 2
    while d < order:
        p = _dot3(p, p)
        t = t + _dot3(t, p)
        d *= 2
    return t


def _gdn_kernel(q_ref, k_ref, v_ref, z_ref, ba_ref, bufq_ref, bufk_ref, bufv_ref, s0_ref,
                cwq_ref, cwk_ref, cwv_ref, alog_ref, dtb_ref, ng_ref,
                o_ref, nbq_ref, nbk_ref, nbv_ref, s_ref, xpq_ref, xpk_ref, xpv_ref, *, c):
    j = pl.program_id(2)
    cp = max(c, SUBLANES) if c < GDN_CP else GDN_CP
    cp = 1 << (cp - 1).bit_length()

    @pl.when(j == 0)
    def _():
        s_ref[...] = s0_ref[...]

    def conv(xp_ref, x_in, buf, cw, nb_ref):
        @pl.when(j == 0)
        def _():
            xp_ref[:, SUBLANES - (CONV_W - 1):SUBLANES, :] = buf[...]

        @pl.when(j > 0)
        def _():
            xp_ref[:, SUBLANES - (CONV_W - 1):SUBLANES, :] = xp_ref[:, c + SUBLANES - (CONV_W - 1):c + SUBLANES, :]

        xp_ref[:, SUBLANES:SUBLANES + c, :] = x_in[...]

        @pl.when(j == pl.num_programs(2) - 1)
        def _():
            nb_ref[...] = xp_ref[:, c + SUBLANES - (CONV_W - 1):c + SUBLANES, :]

        return _silu(_conv_silu_free(xp_ref, cw, None, c)[0])

    nh = q_ref.shape[2] // GDN_HEAD
    q_all = conv(xpq_ref, q_ref, bufq_ref, cwq_ref, nbq_ref)
    k_all = conv(xpk_ref, k_ref, bufk_ref, cwk_ref, nbk_ref)
    v_all = conv(xpv_ref, v_ref, bufv_ref, cwv_ref, nbv_ref)
    nv = nh * GDN_REP
    live = lax.broadcasted_iota(jnp.int32, (cp, LANES), 0) < c
    ti = lax.broadcasted_iota(jnp.int32, (1, cp, cp), 1)
    si = lax.broadcasted_iota(jnp.int32, (1, cp, cp), 2)
    sl = lambda h: slice(h * GDN_HEAD, (h + 1) * GDN_HEAD)
    qs, ks, vs, gcols, bcols, grows = [], [], [], [], [], []
    for hh in range(nh):
        q = _pad_rows(_l2norm(q_all[:, sl(hh)]) * (GDN_HEAD ** -0.5), cp)
        k = _pad_rows(_l2norm(k_all[:, sl(hh)]), cp)
        ba = _pad_rows(ba_ref[0, :, sl(hh)], cp)
        beta = jnp.where(live, _sigmoid(ba), 0.0)
        g = jnp.where(live, -jnp.exp(alog_ref[hh]) * _softplus(ba + dtb_ref[hh]), 0.0)
        gc = _cumsum_time(g)
        gc_t = _transpose_tile(gc)
        for i in range(GDN_REP):
            qs.append(q)
            ks.append(k)
            vs.append(_pad_rows(v_all[:, sl(hh * GDN_REP + i)], cp))
            gcols.append(gc[:, GDN_REP + i:GDN_REP + i + 1])
            bcols.append(beta[:, i:i + 1])
            grows.append(gc_t[GDN_REP + i:GDN_REP + i + 1, :])
    q, k, v = jnp.stack(qs), jnp.stack(ks), jnp.stack(vs)
    gcol, bcol, grow = jnp.stack(gcols), jnp.stack(bcols), jnp.stack(grows)
    glast = gcol[:, cp - 1:cp, :]
    dec_incl = jnp.exp(jnp.where(ti >= si, gcol - grow, -jnp.inf))
    a_mat = bcol * _dot_nt(k, k) * jnp.where(ti > si, dec_incl, 0.0)
    rhs = jnp.concatenate([v * bcol, k * (bcol * jnp.exp(gcol))], axis=-1)
    sol = _dot(_unit_lower_inverse(a_mat), rhs)
    u, w = sol[:, :, :GDN_HEAD], sol[:, :, GDN_HEAD:]
    s = s_ref[0]
    v_new = u - _dot(w, s)
    o = _dot(q * jnp.exp(gcol), s) + _dot(_dot_nt(q, k) * dec_incl, v_new)
    s_ref[0] = s * jnp.exp(glast) + _dot_tn(k * jnp.exp(glast - gcol), v_new)
    o = o[:, :c]
    o = o * lax.rsqrt(jnp.mean(o * o, axis=-1, keepdims=True) + NORM_EPS) * ng_ref[...]
    for vh in range(nv):
        o_ref[:, sl(vh)] = (o[vh] * _silu(z_ref[0, :, sl(vh)])).astype(o_ref.dtype)


def gdn_core(qkvz, ba, conv_buf, s0, conv_w, a_log, dt_bias, norm_g):
    bsz, seqlen, _ = qkvz.shape
    c = min(seqlen, GDN_CP)
    nc = seqlen // c
    kh = GDN_K_HEADS
    nh = GDN_HEADS_PER_STEP if nc > 1 else GDN_HEADS_PER_STEP_SHORT
    hd = nh * GDN_HEAD
    vw = GDN_REP * hd

    def head_pad(v):
        return jnp.pad(v.reshape(kh, 1, GDN_REP), ((0, 0), (0, 0), (GDN_REP, LANES - 2 * GDN_REP)))

    koff = GDN_KEY_DIM // hd
    voff = 2 * GDN_KEY_DIM // vw
    zoff = GDN_CONV_DIM // vw
    seq3 = lambda w, off: pl.BlockSpec((1, c, w), lambda b, h, j: (b, j, h + off))
    buf3 = lambda w, off: pl.BlockSpec((1, CONV_W - 1, w), lambda b, h, j: (b, 0, h + off))
    par2 = lambda w, off: pl.BlockSpec((CONV_W, w), lambda b, h, j: (0, h + off))
    hp = pl.BlockSpec((nh, 1, LANES), lambda b, h, j: (h, 0, 0))
    sspec = pl.BlockSpec((1, nh * GDN_REP, GDN_HEAD, GDN_HEAD), lambda b, h, j: (b, h, 0, 0))
    o, nbq, nbk, nbv, sfin = pl.pallas_call(
        functools.partial(_gdn_kernel, c=c),
        grid=(bsz, kh // nh, nc),
        in_specs=[seq3(hd, 0), seq3(hd, koff), seq3(vw, voff), seq3(vw, zoff), seq3(hd, 0),
                  buf3(hd, 0), buf3(hd, koff), buf3(vw, voff), sspec,
                  par2(hd, 0), par2(hd, koff), par2(vw, voff), hp, hp,
                  pl.BlockSpec((1, GDN_HEAD), lambda b, h, j: (0, 0))],
        out_specs=[pl.BlockSpec((c, vw), lambda b, h, j: (b * nc + j, h)),
                   buf3(hd, 0), buf3(hd, 0), buf3(vw, 0), sspec],
        out_shape=[jax.ShapeDtypeStruct((bsz * seqlen, GDN_VAL_DIM), F32),
                   jax.ShapeDtypeStruct((bsz, CONV_W - 1, GDN_KEY_DIM), F32),
                   jax.ShapeDtypeStruct((bsz, CONV_W - 1, GDN_KEY_DIM), F32),
                   jax.ShapeDtypeStruct((bsz, CONV_W - 1, GDN_VAL_DIM), F32),
                   jax.ShapeDtypeStruct(s0.shape, F32)],
        scratch_shapes=[pltpu.VMEM((1, c + SUBLANES, hd), F32), pltpu.VMEM((1, c + SUBLANES, hd), F32),
                        pltpu.VMEM((1, c + SUBLANES, vw), F32)],
        compiler_params=_cparams("parallel", "parallel", "arbitrary"),
        name="gdn_core",
    )(qkvz, qkvz, qkvz, qkvz, ba, conv_buf, conv_buf, conv_buf, s0,
      conv_w, conv_w, conv_w, head_pad(a_log), head_pad(dt_bias), norm_g.reshape(1, GDN_HEAD))
    return o, jnp.concatenate([nbq, nbk, nbv], axis=-1), sfin


def gdn_ba_weight(w_in):
    tail = w_in[:, GDN_CONV_DIM + GDN_VAL_DIM:]
    b_w = tail[:, :GDN_V_HEADS].reshape(-1, GDN_K_HEADS, GDN_REP)
    a_w = tail[:, GDN_V_HEADS:].reshape(-1, GDN_K_HEADS, GDN_REP)
    blk = jnp.concatenate([b_w, a_w], axis=-1)
    return jnp.pad(blk, ((0, 0), (0, 0), (0, LANES - 2 * GDN_REP))).reshape(-1, GDN_K_HEADS * LANES)


N_EXPERTS = 32
TOP_K = 4
D_FF = 2048
SWIGLU_LIMIT = 7.0
SWIGLU_ALPHA = 1.702
MOE_ROWS = 512
MOE_ROUTER_TOKENS = 512
MOE_TN_UP = 512
MOE_TN_DOWN = 1024
MOE_TOKENS_PER_STEP = 256


def _router_kernel(x_ref, w_ref, b_ref, e_ref, g_ref, r_ref, cnt_ref, xp_ref, carry_ref):
    @pl.when(pl.program_id(0) == 0)
    def _():
        carry_ref[...] = jnp.zeros(carry_ref.shape, F32)

    xp_ref[...] = _pack_rows(x_ref[...])
    logits = _dot6(x_ref[...], w_ref[...]) + b_ref[...]
    tm = logits.shape[0]
    lane = lax.broadcasted_iota(jnp.int32, logits.shape, 1)
    logits = jnp.where(lane < N_EXPERTS, logits, -jnp.inf)
    e_out = jnp.zeros(logits.shape, jnp.int32)
    g_out = jnp.zeros(logits.shape, F32)
    picked = jnp.zeros(logits.shape, F32)
    top, idxs = None, []
    for kth in range(TOP_K):
        m = jnp.max(logits, axis=-1, keepdims=True)
        idx = jnp.min(jnp.where(logits == m, lane, LANES), axis=-1, keepdims=True)
        top = m if top is None else top
        idxs.append(idx)
        e_out = jnp.where(lane == kth, idx, e_out)
        g_out = jnp.where(lane == kth, jnp.exp(m - top), g_out)
        picked = jnp.where(lane == idx, 1.0, picked)
        logits = jnp.where(lane == idx, -jnp.inf, logits)
    e_ref[0] = e_out.T[:SUBLANES]
    g_ref[...] = g_out / jnp.sum(g_out, axis=-1, keepdims=True)
    earlier = (lax.broadcasted_iota(jnp.int32, (tm, tm), 0) > lax.broadcasted_iota(jnp.int32, (tm, tm), 1))
    before = carry_ref[...] + _dot(earlier.astype(F32), picked)
    r_out = jnp.zeros(logits.shape, jnp.int32)
    for kth in range(TOP_K):
        rk = jnp.sum(jnp.where(lane == idxs[kth], before, 0.0), axis=-1, keepdims=True)
        r_out = jnp.where(lane == kth, rk.astype(jnp.int32), r_out)
    r_ref[0] = r_out.T[:SUBLANES]
    carry_ref[...] = carry_ref[...] + jnp.sum(picked, axis=0, keepdims=True)
    cnt_ref[...] = carry_ref[...].astype(jnp.int32)


def moe_router(x, w_router, b_router, layer):
    n_tok, d = x.shape
    tm = MOE_ROUTER_TOKENS
    assert n_tok % tm == 0
    w = jnp.pad(w_router[layer], ((0, 0), (0, LANES - N_EXPERTS)))
    b = jnp.pad(b_router[layer], (0, LANES - N_EXPERTS)).reshape(1, LANES)
    tok = pl.BlockSpec((tm, LANES), lambda i: (i, 0))
    tile = pl.BlockSpec((1, SUBLANES, tm), lambda i: (i, 0, 0))
    return pl.pallas_call(
        _router_kernel,
        grid=(n_tok // tm,),
        in_specs=[pl.BlockSpec((tm, d), lambda i: (i, 0)), pl.BlockSpec((d, LANES), lambda i: (0, 0)),
                  pl.BlockSpec((1, LANES), lambda i: (0, 0))],
        out_specs=[tile, tok, tile, pl.BlockSpec((1, LANES), lambda i: (0, 0)),
                   pl.BlockSpec((tm, d // 2), lambda i: (i, 0))],
        out_shape=[jax.ShapeDtypeStruct((n_tok // tm, SUBLANES, tm), jnp.int32),
                   jax.ShapeDtypeStruct((n_tok, LANES), F32),
                   jax.ShapeDtypeStruct((n_tok // tm, SUBLANES, tm), jnp.int32),
                   jax.ShapeDtypeStruct((1, LANES), jnp.int32),
                   jax.ShapeDtypeStruct((n_tok, d // 2), jnp.uint32)],
        scratch_shapes=[pltpu.VMEM((1, LANES), F32)],
        compiler_params=_cparams("arbitrary"),
        name="moe_router",
    )(x, w, b)


def moe_blocks(counts, n_pairs):
    nb_e = (counts + MOE_ROWS - 1) // MOE_ROWS
    bend = jnp.cumsum(nb_e)
    return nb_e, bend - nb_e, bend, n_pairs // MOE_ROWS + N_EXPERTS


def _issue_and_drain(n_tok, start_one, wait_one):
    def start(t, carry):
        for kth in range(TOP_K):
            start_one(t, kth)
        return carry

    def wait(t, carry):
        for _ in range(TOP_K):
            wait_one()
        return carry

    lax.fori_loop(0, n_tok, start, 0)
    lax.fori_loop(0, n_tok, wait, 0)


def _pair_tile_spec(e_t, tm):
    per = e_t.shape[2] // tm
    return pl.BlockSpec((1, SUBLANES, tm), lambda i, *_: (i // per, 0, i % per), memory_space=pltpu.SMEM)


def _dispatch_kernel(bstart_ref, nb_ref, cnt_ref, e_ref, r_ref, x_ref, xs_ref, zero_ref, sem):
    tm = x_ref.shape[0]

    @pl.when(pl.program_id(0) == 0)
    def _():
        zero_ref[...] = jnp.zeros(zero_ref.shape, zero_ref.dtype)

        def zero_row(row):
            return pltpu.make_async_copy(zero_ref.at[pl.ds(0, 1)], xs_ref.at[pl.ds(row, 1)], sem)

        def zero_piece(row):
            return pltpu.make_async_copy(zero_ref, xs_ref.at[pl.ds(pl.multiple_of(row, SUBLANES), SUBLANES)], sem)

        def drain(n, copy):
            def wait(c, cc):
                copy.wait()
                return cc

            lax.fori_loop(0, n, wait, 0)

        def fill(e, carry):
            first = bstart_ref[e] * MOE_ROWS + cnt_ref[e]
            singles = (-cnt_ref[e]) % SUBLANES
            pieces = (nb_ref[e] * MOE_ROWS - cnt_ref[e] - singles) // SUBLANES

            def start_row(c, cc):
                zero_row(first + c).start()
                return cc

            def start_piece(c, cc):
                zero_piece(first + singles + c * SUBLANES).start()
                return cc

            lax.fori_loop(0, singles, start_row, 0)
            lax.fori_loop(0, pieces, start_piece, 0)
            drain(singles, zero_row(0))
            drain(pieces, zero_piece(0))
            return carry

        lax.fori_loop(0, N_EXPERTS + 1, fill, 0)

    def row_copy(src_row, slot):
        return pltpu.make_async_copy(x_ref.at[pl.ds(src_row, 1)], xs_ref.at[pl.ds(slot, 1)], sem)

    def start_one(t, kth):
        slot = bstart_ref[e_ref[0, kth, t]] * MOE_ROWS + r_ref[0, kth, t]
        row_copy(t, slot).start()

    _issue_and_drain(tm, start_one, lambda: row_copy(0, 0).wait())


def moe_dispatch(x_rows, e_t, r_t, bstart, nb_e, counts, n_blocks):
    n_tok, width = x_rows.shape
    tm = MOE_TOKENS_PER_STEP
    assert n_tok % tm == 0
    smem = _pair_tile_spec(e_t, tm)
    used = bstart[-1:] + nb_e[-1:]
    bstart = jnp.concatenate([bstart, used])
    nb_e = jnp.concatenate([nb_e, n_blocks - used])
    counts = jnp.concatenate([counts, jnp.zeros((1,), counts.dtype)])
    return pl.pallas_call(
        _dispatch_kernel,
        grid_spec=pltpu.PrefetchScalarGridSpec(
            num_scalar_prefetch=3, grid=(n_tok // tm,),
            in_specs=[smem, smem, pl.BlockSpec((tm, width), lambda i, *_: (i, 0))],
            out_specs=pl.BlockSpec(memory_space=pl.ANY),
            scratch_shapes=[pltpu.VMEM((SUBLANES, width), x_rows.dtype), pltpu.SemaphoreType.DMA(())]),
        out_shape=jax.ShapeDtypeStruct((n_blocks * MOE_ROWS, width), x_rows.dtype),
        compiler_params=_cparams("arbitrary"),
        name="moe_dispatch",
    )(bstart, nb_e, counts, e_t, r_t, x_rows)


def _combine_kernel(bstart_ref, e_ref, r_ref, g_ref, ys_ref, o_ref, buf_ref, sem):
    tm = o_ref.shape[0]

    def row_copy(slot, dst_row):
        return pltpu.make_async_copy(ys_ref.at[pl.ds(slot, 1)], buf_ref.at[pl.ds(dst_row, 1)], sem)

    def start_one(t, kth):
        slot = bstart_ref[e_ref[0, kth, t]] * MOE_ROWS + r_ref[0, kth, t]
        row_copy(slot, kth * tm + t).start()

    _issue_and_drain(tm, start_one, lambda: row_copy(0, 0).wait())
    acc = None
    for kth in range(TOP_K):
        term = g_ref[:, kth:kth + 1] * buf_ref[kth * tm:(kth + 1) * tm, :]
        acc = term if acc is None else acc + term
    o_ref[...] = acc


def moe_combine(ys, e_t, r_t, bstart, gates):
    d = ys.shape[1]
    n_tok = gates.shape[0]
    tm = MOE_TOKENS_PER_STEP
    smem = _pair_tile_spec(e_t, tm)
    return pl.pallas_call(
        _combine_kernel,
        grid_spec=pltpu.PrefetchScalarGridSpec(
            num_scalar_prefetch=1, grid=(n_tok // tm,),
            in_specs=[smem, smem, pl.BlockSpec((tm, LANES), lambda i, *_: (i, 0)), pl.BlockSpec(memory_space=pl.ANY)],
            out_specs=pl.BlockSpec((tm, d), lambda i, *_: (i, 0)),
            scratch_shapes=[pltpu.VMEM((TOP_K * tm, d), F32), pltpu.SemaphoreType.DMA(())]),
        out_shape=jax.ShapeDtypeStruct((n_tok, d), F32),
        compiler_params=_cparams("arbitrary"),
        name="moe_combine",
    )(bstart, e_t, r_t, gates, ys)


def moe_items(nb_e, bstart, bend, n_blocks, nt):
    i = jnp.arange(n_blocks * nt, dtype=jnp.int32)
    used = bend[-1]
    e_i = jnp.minimum(jnp.sum((i[:, None] >= (bend * nt)[None, :]).astype(jnp.int32), axis=1), N_EXPERTS - 1)
    local = i - bstart[e_i] * nt
    nbe = jnp.maximum(nb_e[e_i], 1)
    valid = i < used * nt
    spare = i - used * nt
    blk = jnp.where(valid, bstart[e_i] + local % nbe, used + spare // nt)
    blk_in = jnp.where(valid, blk, 0)
    n_out = jnp.where(valid, local // nbe, spare % nt)
    e_last = jnp.max(jnp.where(nb_e > 0, jnp.arange(N_EXPERTS, dtype=jnp.int32), 0))
    e_w = jnp.where(valid, e_i, e_last)
    n_w = jnp.where(valid, local // nbe, nt - 1)
    first = (valid & (local % nbe == 0)).astype(jnp.int32)
    i32 = lambda a: a.astype(jnp.int32)
    return i32(blk_in), i32(blk), i32(e_w), i32(n_w), i32(n_out), first, i32(valid)


def _unpack_rows(words):
    lo = lax.bitcast_convert_type(words << 16, F32)
    hi = lax.bitcast_convert_type(words & jnp.uint32(0xFFFF0000), F32)
    return jnp.concatenate([lo, hi], axis=-1).astype(BF16)


def _pack_rows(x):
    half = x.shape[-1] // 2
    bits = lambda v: lax.bitcast_convert_type(v.astype(BF16).astype(F32), jnp.uint32)
    return (bits(x[:, half:]) & jnp.uint32(0xFFFF0000)) | (bits(x[:, :half]) >> 16)


def _gmm_kernel(bin_ref, blk_ref, ew_ref, nw_ref, no_ref, first_ref, valid_ref, x_ref, *rest, glu):
    i = pl.program_id(0)
    if glu:
        wg_ref, wu_ref, bg_ref, bu_ref, o_ref, wgb_ref, wub_ref = rest
    else:
        w_ref, b_ref, o_ref, wb_ref = rest

    @pl.when(first_ref[i] == 1)
    def _():
        if glu:
            wgb_ref[...] = wg_ref[...].astype(BF16)
            wub_ref[...] = wu_ref[...].astype(BF16)
        else:
            wb_ref[...] = w_ref[...].astype(BF16)

    @pl.when(valid_ref[i] == 1)
    def _():
        if glu:
            x = _unpack_rows(x_ref[...])
            gate = jnp.dot(x, wgb_ref[...], preferred_element_type=F32) + bg_ref[...]
            up = jnp.dot(x, wub_ref[...], preferred_element_type=F32) + bu_ref[...]
            gate = jnp.minimum(gate, SWIGLU_LIMIT)
            up = jnp.clip(up, -SWIGLU_LIMIT, SWIGLU_LIMIT)
            out = (up + 1.0) * (gate * _sigmoid(SWIGLU_ALPHA * gate))
        else:
            out = jnp.dot(x_ref[...], wb_ref[...], preferred_element_type=F32) + b_ref[...]
        o_ref[...] = out.astype(o_ref.dtype)

    @pl.when(valid_ref[i] == 0)
    def _():
        o_ref[...] = jnp.zeros(o_ref.shape, o_ref.dtype)


def grouped_matmul(xs, w, b, layer, items, *, glu):
    k = w.shape[-2]
    n_blocks = xs.shape[0] // MOE_ROWS
    n_out = w.shape[-1] // 2 if glu else w.shape[-1]
    tn = MOE_TN_UP if glu else MOE_TN_DOWN
    n_items = items[0].shape[0]
    b4 = b.reshape(b.shape[0], b.shape[1], 1, b.shape[2])
    wspec = lambda off: pl.BlockSpec((None, None, k, tn),
                                     lambda i, bi, blk, ew, nw, no, fi, va: (layer, ew[i], 0, nw[i] + off))
    bspec = lambda off: pl.BlockSpec((None, None, 1, tn),
                                     lambda i, bi, blk, ew, nw, no, fi, va: (layer, ew[i], 0, nw[i] + off))
    xspec = pl.BlockSpec((MOE_ROWS, xs.shape[1]), lambda i, bi, blk, ew, nw, no, fi, va: (bi[i], 0))
    ospec = pl.BlockSpec((MOE_ROWS, tn), lambda i, bi, blk, ew, nw, no, fi, va: (blk[i], no[i]))
    if glu:
        half = n_out // tn
        in_specs = [xspec, wspec(0), wspec(half), bspec(0), bspec(half)]
        args = (xs, w, w, b4, b4)
        scratch = [pltpu.VMEM((k, tn), BF16)] * 2
    else:
        in_specs = [xspec, wspec(0), bspec(0)]
        args = (xs, w, b4)
        scratch = [pltpu.VMEM((k, tn), BF16)]
    return pl.pallas_call(
        functools.partial(_gmm_kernel, glu=glu),
        grid_spec=pltpu.PrefetchScalarGridSpec(
            num_scalar_prefetch=7, grid=(n_items,), in_specs=in_specs, out_specs=ospec, scratch_shapes=scratch),
        out_shape=jax.ShapeDtypeStruct((n_blocks * MOE_ROWS, n_out), BF16 if glu else F32),
        compiler_params=_cparams("arbitrary"),
        name="moe_up" if glu else "moe_down",
    )(*items, *args)


def moe_ffn(hn, layer, w_router, b_router, w_gu, b_gu, w_down, b_down):
    n_tok, d = hn.shape
    e_t, gates, r_t, counts, x_rows = moe_router(hn, w_router, b_router, layer)
    counts = counts[0, :N_EXPERTS]
    nb_e, bstart, bend, n_blocks = moe_blocks(counts, n_tok * TOP_K)
    xs = moe_dispatch(x_rows, e_t, r_t, bstart, nb_e, counts, n_blocks)
    hmid = grouped_matmul(xs, w_gu, b_gu, layer, moe_items(nb_e, bstart, bend, n_blocks, D_FF // MOE_TN_UP), glu=True)
    ys = grouped_matmul(hmid, w_down, b_down, layer,
                        moe_items(nb_e, bstart, bend, n_blocks, D_MODEL // MOE_TN_DOWN), glu=False)
    return moe_combine(ys, e_t, r_t, bstart, gates)


def ssd_mixer(hn, conv_buf, s0, w_in, conv_w, conv_b, dt_bias, a_log, d_skip, norm_g, w_out):
    bsz, seqlen, d = hn.shape
    h2 = hn.reshape(bsz * seqlen, d)
    zxbc = matmul(h2, w_in, w_index=(0,), n_cols=SSD_D_INNER + SSD_CONV_DIM).reshape(bsz, seqlen, -1)
    dt_raw = matmul(h2, ssd_dt_weight(w_in[0])).reshape(bsz, seqlen, -1)
    y, new_buf, s_fin = ssd_core(zxbc, dt_raw, conv_buf, s0, conv_w[0], conv_b[0], dt_bias[0], a_log[0], d_skip[0],
                                 norm_g[0])
    return matmul(y, w_out, w_index=(0,)), new_buf, s_fin


def lru_mixer(hn, conv_buf, h0, w_in, conv_w, conv_b, w_rg, b_rg, w_ig, b_ig, lam, w_out):
    bsz, seqlen, d = hn.shape
    xg = matmul(hn.reshape(bsz * seqlen, d), w_in, w_index=(0,)).reshape(bsz, seqlen, -1)
    y, new_buf, h_fin = lru_core(xg, conv_buf, h0, conv_w[0], conv_b[0], w_rg[0], b_rg[0].reshape(-1), w_ig[0],
                                 b_ig[0].reshape(-1), lam[0])
    return matmul(y, w_out, w_index=(0,)), new_buf, h_fin


def gdn_mixer(hn, conv_buf, s0, w_in, conv_w, a_log, dt_bias, norm_g, w_out):
    bsz, seqlen, d = hn.shape
    h2 = hn.reshape(bsz * seqlen, d)
    qkvz = matmul(h2, w_in, w_index=(0,), n_cols=GDN_CONV_DIM + GDN_VAL_DIM).reshape(bsz, seqlen, -1)
    ba = matmul(h2, gdn_ba_weight(w_in[0])).reshape(bsz, seqlen, -1)
    o, new_buf, s_fin = gdn_core(qkvz, ba, conv_buf, s0, conv_w[0], a_log[0], dt_bias[0], norm_g[0])
    return matmul(o, w_out, w_index=(0,)), new_buf, s_fin


def kernel(x_prompt, x_sample, c_prompt, c_sample, state_ssd, cache_ssd_conv, state_rwkv, cache_rwkv_shift, state_lru, cache_lru_conv, state_gdn, cache_gdn_conv, w_ada, b_ada, norm_g, final_g, ssd_w_in, ssd_conv_w, ssd_conv_b, ssd_dt_bias, ssd_a_log, ssd_d, ssd_norm_g, ssd_w_out, rwkv_mu, rwkv_w_rkv, rwkv_w0, rwkv_w1, rwkv_w2, rwkv_a0, rwkv_a1, rwkv_a2, rwkv_g1, rwkv_g2, rwkv_k_k, rwkv_k_a, rwkv_r_k, rwkv_lnx_g, rwkv_lnx_b, rwkv_w_o, lru_w_in, lru_conv_w, lru_conv_b, lru_w_rg, lru_b_rg, lru_w_ig, lru_b_ig, lru_lambda, lru_w_out, gdn_w_in, gdn_conv_w, gdn_a_log, gdn_dt_bias, gdn_norm_g, gdn_w_out, moe_w_router, moe_b_router, moe_w_gu, moe_b_gu, moe_w_down, moe_b_down):
    assert DEPTH == 4 and w_ada.shape[0] == DEPTH
    d = D_MODEL
    xs = [x_prompt, x_sample]
    n_seq = [x.shape[0] for x in xs]
    n_rows = [x.shape[0] * x.shape[1] for x in xs]
    row0 = [0, n_rows[0]]

    cond = jnp.concatenate([c_prompt, c_sample], axis=0)
    pad = (-cond.shape[0]) % (2 * SUBLANES)
    cond = jnp.pad(cond, ((0, pad), (0, 0)))
    mods = []
    for i in range(DEPTH):
        m = matmul(cond, w_ada, b_ada, w_index=(i,), pre_silu=True)
        mods.append([m[:n_seq[0]].reshape(n_seq[0], 1, 6 * d),
                     m[n_seq[0]:n_seq[0] + n_seq[1]].reshape(n_seq[1], 1, 6 * d)])

    def zeros_like_state(s, bsz):
        return jnp.zeros((bsz,) + s.shape[2:], F32)

    st = {
        "ssd": [zeros_like_state(state_ssd, n_seq[0]), state_ssd[0]],
        "ssd_conv": [zeros_like_state(cache_ssd_conv, n_seq[0]), cache_ssd_conv[0]],
        "rwkv": [zeros_like_state(state_rwkv, n_seq[0]), state_rwkv[0]],
        "rwkv_shift": [zeros_like_state(cache_rwkv_shift, n_seq[0]), cache_rwkv_shift[0]],
        "lru": [zeros_like_state(state_lru, n_seq[0]), state_lru[0]],
        "lru_conv": [zeros_like_state(cache_lru_conv, n_seq[0]), cache_lru_conv[0]],
        "gdn": [zeros_like_state(state_gdn, n_seq[0]), state_gdn[0]],
        "gdn_conv": [zeros_like_state(cache_gdn_conv, n_seq[0]), cache_gdn_conv[0]],
    }
    new = {}

    hn = [norm_mod(xs[gi], norm_g[0, 0], mods[0][gi], 0) for gi in range(2)]
    outs = None
    for i in range(DEPTH):
        ys = []
        for gi in range(2):
            if i == 0:
                y, buf, s = ssd_mixer(hn[gi], st["ssd_conv"][gi], st["ssd"][gi], ssd_w_in, ssd_conv_w, ssd_conv_b,
                                      ssd_dt_bias, ssd_a_log, ssd_d, ssd_norm_g, ssd_w_out)
                new.setdefault("ssd", []).append(s)
                new.setdefault("ssd_conv", []).append(buf)
            elif i == 1:
                y, buf, s = rwkv_mixer(hn[gi], st["rwkv_shift"][gi], st["rwkv"][gi], rwkv_mu[0], rwkv_w_rkv[0],
                                       rwkv_w0[0], rwkv_w1[0], rwkv_w2[0], rwkv_a0[0], rwkv_a1[0], rwkv_a2[0],
                                       rwkv_g1[0], rwkv_g2[0], rwkv_k_k[0], rwkv_k_a[0], rwkv_r_k[0],
                                       rwkv_lnx_g[0], rwkv_lnx_b[0], rwkv_w_o[0])
                new.setdefault("rwkv", []).append(s)
                new.setdefault("rwkv_shift", []).append(buf)
            elif i == 2:
                y, buf, s = lru_mixer(hn[gi], st["lru_conv"][gi], st["lru"][gi], lru_w_in, lru_conv_w, lru_conv_b,
                                      lru_w_rg, lru_b_rg, lru_w_ig, lru_b_ig, lru_lambda, lru_w_out)
                new.setdefault("lru", []).append(s)
                new.setdefault("lru_conv", []).append(buf)
            else:
                y, buf, s = gdn_mixer(hn[gi], st["gdn_conv"][gi], st["gdn"][gi], gdn_w_in, gdn_conv_w, gdn_a_log,
                                      gdn_dt_bias, gdn_norm_g, gdn_w_out)
                new.setdefault("gdn", []).append(s)
                new.setdefault("gdn_conv", []).append(buf)
            ys.append(y)
        hn2 = []
        for gi in range(2):
            xs[gi], h2 = resid_norm(xs[gi], ys[gi], 0, norm_g[i, 1], mods[i][gi], 0, mods[i][gi], 1)
            hn2.append(h2.reshape(n_rows[gi], d))
        moe = moe_ffn(jnp.concatenate(hn2, axis=0), i, moe_w_router, moe_b_router, moe_w_gu, moe_b_gu, moe_w_down,
                      moe_b_down)
        if i + 1 < DEPTH:
            for gi in range(2):
                xs[gi], hn[gi] = resid_norm(xs[gi], moe, row0[gi], norm_g[i + 1, 0], mods[i][gi], 1,
                                            mods[i + 1][gi], 0)
        else:
            outs = [resid_final(xs[gi], moe, row0[gi], final_g, mods[i][gi], 1) for gi in range(2)]

    res = [outs[0], outs[1]]
    for key in ("ssd", "ssd_conv", "rwkv", "rwkv_shift", "lru", "lru_conv", "gdn", "gdn_conv"):
        res += [new[key][0][None], new[key][1][None]]
    return tuple(res)
```

```python
import functools
import math

import jax
import jax.numpy as jnp
from jax import lax
from jax.experimental import pallas as pl
from jax.experimental.pallas import tpu as pltpu

F32 = jnp.float32
BF16 = jnp.bfloat16

VMEM_LIMIT_BYTES = 56 * 1024 * 1024
LANES = 128
SUBLANES = 8

D_MODEL = 2048
DEPTH = 4
CONV_W = 4
NORM_EPS = 1e-6
L2_EPS = 1e-6

LRU_BLOCKS = 8
LRU_C = 8.0
ADA_TN = 2048


def _cparams(*sem):
    return pltpu.CompilerParams(dimension_semantics=sem, vmem_limit_bytes=VMEM_LIMIT_BYTES)


def _sigmoid(x):
    return 1.0 / (1.0 + jnp.exp(-x))


def _silu(x):
    return x * _sigmoid(x)


def _softplus(x):
    return jnp.maximum(x, 0.0) + jnp.log(1.0 + jnp.exp(-jnp.abs(x)))


def _expm1(x):
    u = jnp.exp(x)
    safe = jnp.where(u == 1.0, 2.0, u)
    return jnp.where(u == 1.0, x, (u - 1.0) * x / jnp.log(safe))


def _dims(a, ca, cb):
    lead = a.ndim - 2
    batch = tuple(range(lead))
    return (((ca + lead,), (cb + lead,)), (batch, batch))


def _dot(a, b):
    return lax.dot_general(a.astype(BF16), b.astype(BF16), _dims(a, 1, 0), preferred_element_type=F32)


def _dot_nt(a, b):
    return lax.dot_general(a.astype(BF16), b.astype(BF16), _dims(a, 1, 1), preferred_element_type=F32)


def _dot_tn(a, b):
    return lax.dot_general(a.astype(BF16), b.astype(BF16), _dims(a, 0, 0), preferred_element_type=F32)


def _split3(a):
    hi = a.astype(BF16)
    r1 = a - hi.astype(F32)
    mid = r1.astype(BF16)
    lo = (r1 - mid.astype(F32)).astype(BF16)
    return hi, mid, lo


def _dot3(a, b):
    ah, am, _ = _split3(a)
    bh, bm, _ = _split3(b)
    return (_dot(ah, bm) + _dot(am, bh)) + _dot(ah, bh)


def _dot6(a, b):
    ah, am, al = _split3(a)
    bh, bm, bl = _split3(b)
    dot = lambda x, y: jnp.dot(x, y, preferred_element_type=F32)
    small = dot(am, bm) + dot(ah, bl) + dot(al, bh)
    return (small + dot(ah, bm) + dot(am, bh)) + dot(ah, bh)


def _mm_kernel(x_ref, w_ref, *rest, has_bias, pre_silu):
    if has_bias:
        b_ref, o_ref, xb_ref = rest
    else:
        o_ref, xb_ref = rest

    @pl.when(pl.program_id(1) == 0)
    def _():
        x = x_ref[...]
        xb_ref[...] = (_silu(x) if pre_silu else x).astype(BF16)

    acc = jnp.dot(xb_ref[...], w_ref[...].astype(BF16), preferred_element_type=F32)
    if has_bias:
        acc = acc + b_ref[...]
    o_ref[...] = acc.astype(o_ref.dtype)


def _pick_tile(n, pref):
    t = min(n, pref)
    while n % t:
        t //= 2
    return t


def matmul(x, w, b=None, *, w_index=(), n_cols=None, col0=0, out_dtype=F32, tm=None, tn=None, pre_silu=False):
    m, k = x.shape
    lead = len(w_index)
    assert w.shape[lead] == k
    n_total = w.shape[lead + 1]
    n = n_total if n_cols is None else n_cols
    if tm is None:
        tm = _pick_tile(m, 1024 if k <= 2048 else 512)
    if tn is None:
        tn = _pick_tile(n, 512)
    assert m % tm == 0 and n % tn == 0 and col0 % tn == 0
    cb = col0 // tn
    widx = tuple(w_index)
    in_specs = [
        pl.BlockSpec((tm, k), lambda i, j: (i, 0)),
        pl.BlockSpec((None,) * lead + (k, tn), lambda i, j: widx + (0, j + cb)),
    ]
    args = [x, w]
    if b is not None:
        bb = b.reshape(b.shape[:lead] + (1, n_total))
        in_specs.append(pl.BlockSpec((None,) * lead + (1, tn), lambda i, j: widx + (0, j + cb)))
        args.append(bb)
    return pl.pallas_call(
        functools.partial(_mm_kernel, has_bias=b is not None, pre_silu=pre_silu),
        grid=(m // tm, n // tn),
        in_specs=in_specs,
        out_specs=pl.BlockSpec((tm, tn), lambda i, j: (i, j)),
        out_shape=jax.ShapeDtypeStruct((m, n), out_dtype),
        scratch_shapes=[pltpu.VMEM((tm, k), BF16)],
        compiler_params=_cparams("parallel", "arbitrary"),
        name="matmul",
    )(*args)


def _rms(x, g):
    return x * lax.rsqrt(jnp.mean(x * x, axis=-1, keepdims=True) + NORM_EPS) * g


def _norm_mod_kernel(x_ref, g_ref, sh_ref, sc_ref, o_ref):
    y = _rms(x_ref[...], g_ref[...])
    o_ref[...] = y * (1.0 + sc_ref[...]) + sh_ref[...]


def _seq_blocks(bsz, seqlen):
    if seqlen >= 256:
        return 1, 256
    return max(1, 128 // seqlen), seqlen


def _mod_spec(bb, col):
    return pl.BlockSpec((bb, 1, D_MODEL), lambda i, j: (i, 0, col))


def norm_mod(x, g, mod, which):
    bsz, seqlen, d = x.shape
    bb, bl = _seq_blocks(bsz, seqlen)
    xspec = pl.BlockSpec((bb, bl, d), lambda i, j: (i, j, 0))
    return pl.pallas_call(
        _norm_mod_kernel,
        grid=(bsz // bb, seqlen // bl),
        in_specs=[xspec, pl.BlockSpec((1, 1, d), lambda i, j: (0, 0, 0)),
                  _mod_spec(bb, 3 * which), _mod_spec(bb, 3 * which + 1)],
        out_specs=xspec,
        out_shape=jax.ShapeDtypeStruct(x.shape, F32),
        compiler_params=_cparams("parallel", "parallel"),
        name="norm_mod",
    )(x, g.reshape(1, 1, d), mod, mod)


def _resid_norm_kernel(x_ref, y_ref, gt_ref, g_ref, sh_ref, sc_ref, xo_ref, ho_ref):
    xn = x_ref[...] + gt_ref[...] * y_ref[...]
    xo_ref[...] = xn
    ho_ref[...] = _rms(xn, g_ref[...]) * (1.0 + sc_ref[...]) + sh_ref[...]


def _rows_view(y, row0, bsz, seqlen, bb, bl):
    d = y.shape[-1]
    assert bb == 1 or bl == seqlen
    assert row0 % (bb * bl) == 0
    base, per_seq = row0 // (bb * bl), seqlen // bl
    return y.reshape(-1, bl, d), pl.BlockSpec((bb, bl, d), lambda i, j: (base + i * per_seq + j, 0, 0))


def resid_norm(x, y, row0, g, mod_gate, which_gate, mod_norm, which_norm):
    bsz, seqlen, d = x.shape
    bb, bl = _seq_blocks(bsz, seqlen)
    xspec = pl.BlockSpec((bb, bl, d), lambda i, j: (i, j, 0))
    y3, yspec = _rows_view(y, row0, bsz, seqlen, bb, bl)
    return pl.pallas_call(
        _resid_norm_kernel,
        grid=(bsz // bb, seqlen // bl),
        in_specs=[xspec, yspec, _mod_spec(bb, 3 * which_gate + 2),
                  pl.BlockSpec((1, 1, d), lambda i, j: (0, 0, 0)),
                  _mod_spec(bb, 3 * which_norm), _mod_spec(bb, 3 * which_norm + 1)],
        out_specs=[xspec, xspec],
        out_shape=[jax.ShapeDtypeStruct(x.shape, F32)] * 2,
        compiler_params=_cparams("parallel", "parallel"),
        name="resid_norm",
    )(x, y3, mod_gate, g.reshape(1, 1, d), mod_norm, mod_norm)


def _resid_final_kernel(x_ref, y_ref, gt_ref, g_ref, o_ref):
    o_ref[...] = _rms(x_ref[...] + gt_ref[...] * y_ref[...], g_ref[...])


def resid_final(x, y, row0, g, mod_gate, which_gate):
    bsz, seqlen, d = x.shape
    bb, bl = _seq_blocks(bsz, seqlen)
    xspec = pl.BlockSpec((bb, bl, d), lambda i, j: (i, j, 0))
    y3, yspec = _rows_view(y, row0, bsz, seqlen, bb, bl)
    return pl.pallas_call(
        _resid_final_kernel,
        grid=(bsz // bb, seqlen // bl),
        in_specs=[xspec, yspec, _mod_spec(bb, 3 * which_gate + 2),
                  pl.BlockSpec((1, 1, d), lambda i, j: (0, 0, 0))],
        out_specs=xspec,
        out_shape=jax.ShapeDtypeStruct(x.shape, F32),
        compiler_params=_cparams("parallel", "parallel"),
        name="resid_final",
    )(x, y3, mod_gate, g.reshape(1, 1, d))


def _shift_time(x, d, fill, axis):
    t = lax.broadcasted_iota(jnp.int32, x.shape, axis)
    return jnp.where(t >= d, pltpu.roll(x, d, axis), fill)


def _conv_silu_free(xp_ref, w_ref, b, tlen):
    acc = None
    for k in range(CONV_W):
        term = xp_ref[:, pl.ds(SUBLANES - (CONV_W - 1) + k, tlen), :] * w_ref[k:k + 1, :]
        acc = term if acc is None else acc + term
    if b is not None:
        acc = acc + b
    return acc


def _load_history(xp_ref, x_ref, buf_ref, tlen):
    j = pl.program_id(1)

    @pl.when(j == 0)
    def _():
        xp_ref[:, SUBLANES - (CONV_W - 1):SUBLANES, :] = buf_ref[...]

    @pl.when(j > 0)
    def _():
        xp_ref[:, SUBLANES - (CONV_W - 1):SUBLANES, :] = xp_ref[:, tlen + SUBLANES - (CONV_W - 1):tlen + SUBLANES, :]

    xp_ref[:, SUBLANES:SUBLANES + tlen, :] = x_ref[...]


def _gelu_tanh(x):
    return 0.5 * x * (1.0 + jnp.tanh(math.sqrt(2.0 / math.pi) * (x + 0.044715 * (x * x * x))))


def _lru_kernel(xb_ref, gb_ref, buf_ref, h0_ref, cw_ref, cb_ref, wrg_ref, brg_ref, wig_ref, big_ref, lam_ref,
                y_ref, nbuf_ref, hfin_ref, xp_ref, h_ref, *, tlen):
    j = pl.program_id(1)
    bb = xb_ref.shape[0]
    width = xb_ref.shape[2]
    blk = width // LRU_BLOCKS

    @pl.when(j == 0)
    def _():
        h_ref[...] = h0_ref[...]

    _load_history(xp_ref, xb_ref, buf_ref, tlen)
    xc = _conv_silu_free(xp_ref, cw_ref, cb_ref[...], tlen)
    x2 = xc.reshape(bb * tlen, width)
    gr, gi = [], []
    for h in range(LRU_BLOCKS):
        xh = x2[:, h * blk:(h + 1) * blk]
        gr.append(_dot(xh, wrg_ref[h]))
        gi.append(_dot(xh, wig_ref[h]))
    gate_r = _sigmoid(jnp.concatenate(gr, axis=-1) + brg_ref[...]).reshape(bb, tlen, width)
    gate_i = _sigmoid(jnp.concatenate(gi, axis=-1) + big_ref[...]).reshape(bb, tlen, width)
    log_a = (-LRU_C) * gate_r * _softplus(-lam_ref[...])
    a = jnp.exp(log_a)
    bx = jnp.sqrt(-_expm1(2.0 * log_a)) * gate_i * xc
    d = 1
    while d < tlen:
        bx = a * _shift_time(bx, d, 0.0, 1) + bx
        a = a * _shift_time(a, d, 1.0, 1)
        d *= 2
    hs = bx + a * h_ref[...]
    h_ref[...] = hs[:, tlen - 1:tlen, :]
    y = hs * _gelu_tanh(gb_ref[...])
    y_ref[...] = y.reshape(bb * tlen, width).astype(y_ref.dtype)

    @pl.when(j == pl.num_programs(1) - 1)
    def _():
        nbuf_ref[...] = xp_ref[:, tlen + SUBLANES - (CONV_W - 1):tlen + SUBLANES, :]
        hfin_ref[...] = hs[:, tlen - 1:tlen, :]


def lru_core(xg, conv_buf, h0, conv_w, conv_b, w_rg, b_rg, w_ig, b_ig, lam):
    bsz, seqlen, w2 = xg.shape
    width = w2 // 2
    bb, tlen = _seq_blocks(bsz, seqlen)
    nj = seqlen // tlen
    row = lambda v: v.reshape(1, 1, width)
    full3 = pl.BlockSpec((1, 1, width), lambda i, j: (0, 0, 0))
    wspec = pl.BlockSpec(w_rg.shape, lambda i, j: (0, 0, 0))
    y, nbuf, hfin = pl.pallas_call(
        functools.partial(_lru_kernel, tlen=tlen),
        grid=(bsz // bb, nj),
        in_specs=[pl.BlockSpec((bb, tlen, width), lambda i, j: (i, j, 0)),
                  pl.BlockSpec((bb, tlen, width), lambda i, j: (i, j, 1)),
                  pl.BlockSpec((bb, CONV_W - 1, width), lambda i, j: (i, 0, 0)),
                  pl.BlockSpec((bb, 1, width), lambda i, j: (i, 0, 0)),
                  pl.BlockSpec((CONV_W, width), lambda i, j: (0, 0)),
                  full3, wspec, full3, wspec, full3, full3],
        out_specs=[pl.BlockSpec((bb * tlen, width), lambda i, j: (i * nj + j, 0)),
                   pl.BlockSpec((bb, CONV_W - 1, width), lambda i, j: (i, 0, 0)),
                   pl.BlockSpec((bb, 1, width), lambda i, j: (i, 0, 0))],
        out_shape=[jax.ShapeDtypeStruct((bsz * seqlen, width), BF16),
                   jax.ShapeDtypeStruct((bsz, CONV_W - 1, width), F32),
                   jax.ShapeDtypeStruct((bsz, 1, width), F32)],
        scratch_shapes=[pltpu.VMEM((bb, tlen + SUBLANES, width), F32), pltpu.VMEM((bb, 1, width), F32)],
        compiler_params=_cparams("parallel", "arbitrary"),
        name="lru_core",
    )(xg, xg, conv_buf, h0.reshape(bsz, 1, width), conv_w, row(conv_b), w_rg, row(b_rg), w_ig, row(b_ig), row(lam))
    return y, nbuf, hfin.reshape(bsz, width)


SSD_HEADDIM = 64
SSD_GROUPS = 8
SSD_HPG = 8
SSD_HEADS = SSD_GROUPS * SSD_HPG
SSD_STATE = 128
SSD_D_INNER = SSD_HEADS * SSD_HEADDIM
SSD_GROUP_W = SSD_HPG * SSD_HEADDIM
SSD_CONV_DIM = SSD_D_INNER + 2 * SSD_GROUPS * SSD_STATE
SSD_QP = 128
SSD_GROUPS_PER_STEP = 4
SSD_GROUPS_PER_STEP_SHORT = 8


def _cumsum_time(x):
    n = x.shape[0]
    d = 1
    while d < n:
        x = x + _shift_time(x, d, 0.0, 0)
        d *= 2
    return x


def _pad_rows(x, rows):
    if x.shape[0] == rows:
        return x
    return jnp.concatenate([x, jnp.zeros((rows - x.shape[0],) + x.shape[1:], x.dtype)], axis=0)


def _lane_half(shape):
    return lax.broadcasted_iota(jnp.int32, shape, 1) < SSD_HEADDIM


def _ssd_kernel(z_ref, x_ref, b_ref, c_ref, dt_ref, bufx_ref, bufb_ref, bufc_ref, s0_ref,
                cwx_ref, cwb_ref, cwc_ref, cbx_ref, cbb_ref, cbc_ref, dtb_ref, alog_ref, dsk_ref, ng_ref,
                y_ref, nbx_ref, nbb_ref, nbc_ref, s_ref, xpx_ref, xpb_ref, xpc_ref, *, q):
    j = pl.program_id(2)
    qp = max(q, SUBLANES)

    @pl.when(j == 0)
    def _():
        s_ref[...] = s0_ref[...]

    def conv(xp_ref, x_in, buf, cw, cb, nb_ref):
        @pl.when(j == 0)
        def _():
            xp_ref[:, SUBLANES - (CONV_W - 1):SUBLANES, :] = buf[...]

        @pl.when(j > 0)
        def _():
            xp_ref[:, SUBLANES - (CONV_W - 1):SUBLANES, :] = xp_ref[:, q + SUBLANES - (CONV_W - 1):q + SUBLANES, :]

        xp_ref[:, SUBLANES:SUBLANES + q, :] = x_in[...]

        @pl.when(j == pl.num_programs(2) - 1)
        def _():
            nb_ref[...] = xp_ref[:, q + SUBLANES - (CONV_W - 1):q + SUBLANES, :]

        return _pad_rows(_silu(_conv_silu_free(xp_ref, cw, cb[...], q)[0]), qp)

    ngrp = x_ref.shape[2] // SSD_GROUP_W
    xs = conv(xpx_ref, x_ref, bufx_ref, cwx_ref, cbx_ref, nbx_ref)
    bm = conv(xpb_ref, b_ref, bufb_ref, cwb_ref, cbb_ref, nbb_ref)
    cm = conv(xpc_ref, c_ref, bufc_ref, cwc_ref, cbc_ref, nbc_ref)

    row = lax.broadcasted_iota(jnp.int32, (qp, LANES), 0)
    causal = lax.broadcasted_iota(jnp.int32, (qp, qp), 0) >= lax.broadcasted_iota(jnp.int32, (qp, qp), 1)
    lane_lo = _lane_half((qp, LANES))
    row_lo = lax.broadcasted_iota(jnp.int32, (LANES, SSD_STATE), 0) < SSD_HEADDIM
    npairs = SSD_HPG // 2
    lanes = lambda t, i: t[:, i * LANES:(i + 1) * LANES]

    def col(v, h, n=LANES):
        return jnp.broadcast_to(v[:, h:h + 1], (qp, n))

    bm3 = jnp.stack([lanes(bm, g) for g in range(ngrp)])
    cm3 = jnp.stack([lanes(cm, g) for g in range(ngrp)])
    cb3 = _dot_nt(cm3, bm3)
    m_head, x_pair, ecum_p, toend_p, el_p = [], [], [], [], []
    for g in range(ngrp):
        dt = jnp.where(row < q, _softplus(_pad_rows(lanes(dt_ref[0], g), qp) + dtb_ref[g]), 0.0)
        cum = _cumsum_time(dt * (-jnp.exp(alog_ref[g])))
        last = cum[qp - 1:qp, :]
        ecum, toend, elast = jnp.exp(cum), jnp.exp(last - cum) * dt, jnp.exp(last)
        cum_t = _transpose_tile(cum)
        dt_t = _transpose_tile(dt)
        for h in range(SSD_HPG):
            seg = col(cum, h, qp) - cum_t[h:h + 1, :]
            m_head.append(cb3[g] * jnp.exp(jnp.where(causal, seg, -jnp.inf)) * dt_t[h:h + 1, :])
        for p in range(npairs):
            x_pair.append(lanes(xs, g * npairs + p))
            ecum_p.append(jnp.where(lane_lo, col(ecum, 2 * p), col(ecum, 2 * p + 1)))
            toend_p.append(jnp.where(lane_lo, col(toend, 2 * p), col(toend, 2 * p + 1)))
            el_p.append(jnp.where(row_lo, jnp.broadcast_to(elast[:, 2 * p:2 * p + 1], (LANES, SSD_STATE)),
                                  jnp.broadcast_to(elast[:, 2 * p + 1:2 * p + 2], (LANES, SSD_STATE))))
    nh, npr = ngrp * SSD_HPG, ngrp * npairs
    x_pair = jnp.stack(x_pair)
    intra = _dot(jnp.stack(m_head), jnp.stack([x_pair[h // 2] for h in range(nh)]))
    s_pair = s_ref[0].reshape(npr, 2 * SSD_HEADDIM, SSD_STATE)
    cm_b = jnp.stack([cm3[i // npairs] for i in range(npr)])
    bm_b = jnp.stack([bm3[i // npairs] for i in range(npr)])
    y = jnp.stack([jnp.where(lane_lo, intra[2 * i], intra[2 * i + 1]) for i in range(npr)])
    y = y + _dot_nt(cm_b, s_pair) * jnp.stack(ecum_p)
    s_new = s_pair * jnp.stack(el_p) + _dot_tn(x_pair * jnp.stack(toend_p), bm_b)
    s_ref[0] = s_new.reshape(nh, SSD_HEADDIM, SSD_STATE)
    y = jnp.concatenate([y[i] for i in range(npr)], axis=-1) + dsk_ref[...] * xs

    yz = y[:q] * _silu(z_ref[0])
    for g in range(ngrp):
        gsl = slice(g * SSD_GROUP_W, (g + 1) * SSD_GROUP_W)
        yg = yz[:, gsl]
        yg = yg * lax.rsqrt(jnp.mean(yg * yg, axis=-1, keepdims=True) + NORM_EPS) * ng_ref[:, gsl]
        y_ref[:, gsl] = yg.astype(y_ref.dtype)


def ssd_core(zxbc, dt_raw, conv_buf, s0, conv_w, conv_b, dt_bias, a_log, d_skip, norm_g):
    bsz, seqlen, _ = zxbc.shape
    q = min(seqlen, SSD_QP)
    nc = seqlen // q
    g8 = SSD_GROUPS

    def grp_pad(v):
        return jnp.pad(v.reshape(g8, 1, SSD_HPG), ((0, 0), (0, 0), (0, LANES - SSD_HPG)))

    d_exp = jnp.repeat(d_skip, SSD_HEADDIM).reshape(1, SSD_D_INNER)
    cb2 = conv_b.reshape(1, SSD_CONV_DIM)
    ngrp = SSD_GROUPS_PER_STEP if nc > 1 else SSD_GROUPS_PER_STEP_SHORT
    xw, sw = ngrp * SSD_GROUP_W, ngrp * SSD_STATE
    xoff = SSD_D_INNER // xw
    boff = 2 * SSD_D_INNER // sw
    coff = boff + SSD_GROUPS // ngrp
    cboff = SSD_D_INNER // sw
    ccoff = cboff + SSD_GROUPS // ngrp
    seq3 = lambda w, off: pl.BlockSpec((1, q, w), lambda b, g, j: (b, j, g + off))
    buf3 = lambda w, off: pl.BlockSpec((1, CONV_W - 1, w), lambda b, g, j: (b, 0, g + off))
    par2 = lambda r, w, off: pl.BlockSpec((r, w), lambda b, g, j: (0, g + off))
    grp = pl.BlockSpec((ngrp, 1, LANES), lambda b, g, j: (g, 0, 0))
    sspec = pl.BlockSpec((1, ngrp * SSD_HPG, SSD_HEADDIM, SSD_STATE), lambda b, g, j: (b, g, 0, 0))
    y, nbx, nbb, nbc, sfin = pl.pallas_call(
        functools.partial(_ssd_kernel, q=q),
        grid=(bsz, g8 // ngrp, nc),
        in_specs=[seq3(xw, 0), seq3(xw, xoff), seq3(sw, boff), seq3(sw, coff),
                  seq3(sw, 0),
                  buf3(xw, 0), buf3(sw, cboff), buf3(sw, ccoff), sspec,
                  par2(CONV_W, xw, 0), par2(CONV_W, sw, cboff), par2(CONV_W, sw, ccoff),
                  par2(1, xw, 0), par2(1, sw, cboff), par2(1, sw, ccoff),
                  grp, grp, par2(1, xw, 0), par2(1, xw, 0)],
        out_specs=[pl.BlockSpec((q, xw), lambda b, g, j: (b * nc + j, g)),
                   buf3(xw, 0), buf3(sw, 0), buf3(sw, 0), sspec],
        out_shape=[jax.ShapeDtypeStruct((bsz * seqlen, SSD_D_INNER), F32),
                   jax.ShapeDtypeStruct((bsz, CONV_W - 1, SSD_D_INNER), F32),
                   jax.ShapeDtypeStruct((bsz, CONV_W - 1, SSD_GROUPS * SSD_STATE), F32),
                   jax.ShapeDtypeStruct((bsz, CONV_W - 1, SSD_GROUPS * SSD_STATE), F32),
                   jax.ShapeDtypeStruct(s0.shape, F32)],
        scratch_shapes=[pltpu.VMEM((1, q + SUBLANES, xw), F32),
                        pltpu.VMEM((1, q + SUBLANES, sw), F32),
                        pltpu.VMEM((1, q + SUBLANES, sw), F32)],
        compiler_params=_cparams("parallel", "parallel", "arbitrary"),
        name="ssd_core",
    )(zxbc, zxbc, zxbc, zxbc, dt_raw, conv_buf, conv_buf, conv_buf, s0,
      conv_w, conv_w, conv_w, cb2, cb2, cb2, grp_pad(dt_bias), grp_pad(a_log), d_exp, norm_g.reshape(1, SSD_D_INNER))
    return y, jnp.concatenate([nbx, nbb, nbc], axis=-1), sfin


def ssd_dt_weight(w_in):
    w_dt = w_in[:, SSD_D_INNER + SSD_CONV_DIM:]
    w_dt = w_dt.reshape(-1, SSD_GROUPS, SSD_HPG)
    return jnp.pad(w_dt, ((0, 0), (0, 0), (0, LANES - SSD_HPG))).reshape(-1, SSD_GROUPS * LANES)


RWKV_HEAD = 64
RWKV_HEADS = D_MODEL // RWKV_HEAD
RWKV_GN_EPS = 64e-5
RWKV_CHUNK = 64
RWKV_PAIRS_PER_STEP = 8
RWKV_PAIRS_PER_STEP_SHORT = 16


def _rwkv_mix_kernel(x_ref, buf_ref, mu_ref, *rest):
    outs, last_ref, carry_ref = rest[:6], rest[6], rest[7]
    j = pl.program_id(1)
    bb, bl, d = x_ref.shape

    @pl.when(j == 0)
    def _():
        carry_ref[...] = buf_ref[...]

    x = x_ref[...]
    t = lax.broadcasted_iota(jnp.int32, x.shape, 1)
    prev = jnp.where(t >= 1, pltpu.roll(x, 1, 1), carry_ref[...])
    carry_ref[...] = x[:, bl - 1:bl, :]
    diff = prev - x
    for s in range(6):
        outs[s][...] = (x + diff * mu_ref[s:s + 1, :]).reshape(bb * bl, d).astype(BF16)

    @pl.when(j == pl.num_programs(1) - 1)
    def _():
        last_ref[...] = x[:, bl - 1:bl, :]


def rwkv_mix(x, shift_buf, mu):
    bsz, seqlen, d = x.shape
    bb, bl = _seq_blocks(bsz, seqlen)
    nj = seqlen // bl
    ospec = pl.BlockSpec((bb * bl, d), lambda i, j: (i * nj + j, 0))
    cspec = pl.BlockSpec((bb, 1, d), lambda i, j: (i, 0, 0))
    outs = pl.pallas_call(
        _rwkv_mix_kernel,
        grid=(bsz // bb, nj),
        in_specs=[pl.BlockSpec((bb, bl, d), lambda i, j: (i, j, 0)), cspec,
                  pl.BlockSpec((6, d), lambda i, j: (0, 0))],
        out_specs=[ospec] * 6 + [cspec],
        out_shape=[jax.ShapeDtypeStruct((bsz * seqlen, d), BF16)] * 6 + [jax.ShapeDtypeStruct((bsz, 1, d), F32)],
        scratch_shapes=[pltpu.VMEM((bb, 1, d), F32)],
        compiler_params=_cparams("parallel", "arbitrary"),
        name="rwkv_mix",
    )(x, shift_buf.reshape(bsz, 1, d), mu)
    return outs[:6], outs[6].reshape(bsz, d)


def _lora_kernel(x_ref, w1_ref, w2_ref, *rest, act, has_bias):
    if has_bias:
        b_ref, o_ref = rest
    else:
        (o_ref,) = rest
    hmid = _dot(x_ref[...], w1_ref[...])
    if act == "tanh":
        hmid = jnp.tanh(hmid)
    elif act == "sigmoid":
        hmid = _sigmoid(hmid)
    out = _dot(hmid, w2_ref[...])
    if has_bias:
        out = out + b_ref[...]
    o_ref[...] = out


def lora(x, w1, w2, bias, act):
    m, k = x.shape
    r = w1.shape[1]
    n = w2.shape[1]
    tm = _pick_tile(m, 512)
    in_specs = [pl.BlockSpec((tm, k), lambda i: (i, 0)), pl.BlockSpec((k, r), lambda i: (0, 0)),
                pl.BlockSpec((r, n), lambda i: (0, 0))]
    args = [x, w1, w2]
    if bias is not None:
        in_specs.append(pl.BlockSpec((1, n), lambda i: (0, 0)))
        args.append(bias.reshape(1, n))
    return pl.pallas_call(
        functools.partial(_lora_kernel, act=act, has_bias=bias is not None),
        grid=(m // tm,),
        in_specs=in_specs,
        out_specs=pl.BlockSpec((tm, n), lambda i: (i, 0)),
        out_shape=jax.ShapeDtypeStruct((m, n), F32),
        compiler_params=_cparams("parallel"),
        name="lora",
    )(*args)


def _half_sum(x, lo):
    s0 = jnp.sum(jnp.where(lo, x, 0.0), axis=-1, keepdims=True)
    s1 = jnp.sum(jnp.where(lo, 0.0, x), axis=-1, keepdims=True)
    return jnp.where(lo, s0, s1)


def _stack_heads(x, lo):
    return jnp.concatenate([jnp.where(lo, x, 0.0), jnp.where(lo, 0.0, x)], axis=0)


def _rwkv_kernel(r_ref, k_ref, v_ref, w_ref, a_ref, g_ref, s0_ref, kk_ref, ka_ref, rk_ref, lg_ref, lb_ref,
                 y_ref, sfin_ref, st_ref, *, c):
    j = pl.program_id(2)
    hd = RWKV_HEAD
    npair = r_ref.shape[2] // LANES
    lo = lax.broadcasted_iota(jnp.int32, (c, LANES), 1) < hd
    zero = jnp.zeros((hd, hd), F32)
    pl_ = lambda p: slice(p * LANES, (p + 1) * LANES)

    @pl.when(j == 0)
    def _():
        for p in range(npair):
            st_ref[p] = jnp.concatenate([jnp.concatenate([s0_ref[0, 2 * p], zero], axis=1),
                                         jnp.concatenate([zero, s0_ref[0, 2 * p + 1]], axis=1)], axis=0)

    def pairwise(fn, *xs):
        return jnp.concatenate([fn(*[x[:, pl_(p)] for x in xs]) for p in range(npair)], axis=-1)

    half_sum = lambda x: pairwise(lambda t: _half_sum(t, lo), x)
    r, k, v = r_ref[0], k_ref[0], v_ref[0]
    lw = -jnp.exp(-_softplus(-w_ref[0]) - 0.5)
    a = _sigmoid(a_ref[0])
    kkr = k * kk_ref[...]
    kk = kkr * lax.rsqrt(half_sum(kkr * kkr) + L2_EPS)
    k = k * (1.0 + (a - 1.0) * ka_ref[...])
    cum = _cumsum_time(lw)
    gam = jnp.exp(cum)
    inv = jnp.exp(-cum)
    stack = lambda x: jnp.stack([_stack_heads(x[:, pl_(p)], lo) for p in range(npair)])
    glast = jnp.stack([gam[c - 1:c, pl_(p)] for p in range(npair)])
    ka_s = stack(jnp.exp(cum - lw) * kk)
    al_s = stack(kk * a * inv)
    k_s = stack(k * inv)
    r_s = stack(r * gam)
    v_s = stack(v)

    n = 2 * c
    ti = lax.broadcasted_iota(jnp.int32, (1, n, n), 1) % c
    si = lax.broadcasted_iota(jnp.int32, (1, n, n), 2) % c
    strict, incl = ti > si, ti >= si
    st = st_ref[...]
    a_mat = jnp.where(strict, _dot_nt(ka_s, al_s), 0.0)
    b_mat = jnp.where(strict, _dot_nt(ka_s, k_s), 0.0)
    d = _dot(_unit_lower_inverse(a_mat, c), _dot_nt(ka_s, st) + _dot(b_mat, v_s))
    o = (_dot_nt(r_s, st) - _dot(jnp.where(incl, _dot_nt(r_s, al_s), 0.0), d)
         + _dot(jnp.where(incl, _dot_nt(r_s, k_s), 0.0), v_s))
    st_ref[...] = st * glast + _dot_tn(v_s, k_s * glast) - _dot_tn(d, al_s * glast)

    o = o[:, :c] + o[:, c:]
    o = jnp.concatenate([o[p] for p in range(npair)], axis=-1)
    mean = half_sum(o) * (1.0 / hd)
    var = half_sum(jnp.square(o - mean)) * (1.0 / hd)
    o = (o - mean) * lax.rsqrt(var + RWKV_GN_EPS) * lg_ref[...] + lb_ref[...]
    o = o + half_sum(r * k * rk_ref[...]) * v
    y_ref[...] = (o * g_ref[0]).astype(y_ref.dtype)

    @pl.when(j == pl.num_programs(2) - 1)
    def _():
        for p in range(npair):
            fin = st_ref[p]
            sfin_ref[0, 2 * p] = fin[:hd, :hd]
            sfin_ref[0, 2 * p + 1] = fin[hd:, hd:]


def rwkv_core(r, k, v, w_raw, a_pre, g, s0, k_k, k_a, r_k, lnx_g, lnx_b):
    bsz, seqlen, d = r.shape
    c = min(seqlen, RWKV_CHUNK)
    nc = seqlen // c
    npair = RWKV_PAIRS_PER_STEP if nc > 1 else RWKV_PAIRS_PER_STEP_SHORT
    wid = npair * LANES
    seq = pl.BlockSpec((1, c, wid), lambda b, p, j: (b, j, p))
    par = pl.BlockSpec((1, wid), lambda b, p, j: (0, p))
    sspec = pl.BlockSpec((1, 2 * npair, RWKV_HEAD, RWKV_HEAD), lambda b, p, j: (b, p, 0, 0))
    row = lambda t: t.reshape(1, d)
    return pl.pallas_call(
        functools.partial(_rwkv_kernel, c=c),
        grid=(bsz, d // wid, nc),
        in_specs=[seq] * 6 + [sspec] + [par] * 5,
        out_specs=[pl.BlockSpec((c, wid), lambda b, p, j: (b * nc + j, p)), sspec],
        out_shape=[jax.ShapeDtypeStruct((bsz * seqlen, d), F32), jax.ShapeDtypeStruct(s0.shape, F32)],
        scratch_shapes=[pltpu.VMEM((npair, LANES, LANES), F32)],
        compiler_params=_cparams("parallel", "parallel", "arbitrary"),
        name="rwkv_core",
    )(r, k, v, w_raw, a_pre, g, s0, row(k_k), row(k_a), row(r_k), row(lnx_g), row(lnx_b))


def rwkv_mixer(hn, shift_buf, s0, mu, w_rkv, w0, w1, w2, a0, a1, a2, g1, g2, k_k, k_a, r_k, lnx_g, lnx_b, w_o):
    bsz, seqlen, d = hn.shape
    (xr, xk, xv, xw, xa, xg), new_shift = rwkv_mix(hn, shift_buf, mu)
    as3 = lambda t: t.reshape(bsz, seqlen, d)
    r, k, v = (as3(matmul(x, w_rkv, w_index=(s,))) for s, x in enumerate((xr, xk, xv)))
    w_raw = as3(lora(xw, w1, w2, w0, "tanh"))
    a_pre = as3(lora(xa, a1, a2, a0, "none"))
    g = as3(lora(xg, g1, g2, None, "sigmoid"))
    y, s_fin = rwkv_core(r, k, v, w_raw, a_pre, g, s0, k_k, k_a, r_k.reshape(-1), lnx_g, lnx_b)
    return matmul(y, w_o), new_shift, s_fin


GDN_K_HEADS = 16
GDN_V_HEADS = 32
GDN_REP = GDN_V_HEADS // GDN_K_HEADS
GDN_HEAD = 128
GDN_KEY_DIM = GDN_K_HEADS * GDN_HEAD
GDN_VAL_DIM = GDN_V_HEADS * GDN_HEAD
GDN_CONV_DIM = 2 * GDN_KEY_DIM + GDN_VAL_DIM
GDN_CP = 64
GDN_HEADS_PER_STEP = 8
GDN_HEADS_PER_STEP_SHORT = 16


def _transpose_tile(x):
    r = x.shape[0]
    return _pad_rows(x, LANES).T[:, :r]


def _l2norm(x):
    return x * lax.rsqrt(jnp.sum(x * x, axis=-1, keepdims=True) + L2_EPS)


def _unit_lower_inverse(a_strict, order=None):
    n = a_strict.shape[-1]
    order = n if order is None else order
    eye = (lax.broadcasted_iota(jnp.int32, (n, n), 0) == lax.broadcasted_iota(jnp.int32, (n, n), 1)).astype(F32)
    p = -a_strict
    t = eye + p
    d = 2
    while d < order:
        p = _dot3(p, p)
        t = t + _dot3(t, p)
        d *= 2
    return t


def _gdn_kernel(q_ref, k_ref, v_ref, z_ref, ba_ref, bufq_ref, bufk_ref, bufv_ref, s0_ref,
                cwq_ref, cwk_ref, cwv_ref, alog_ref, dtb_ref, ng_ref,
                o_ref, nbq_ref, nbk_ref, nbv_ref, s_ref, xpq_ref, xpk_ref, xpv_ref, *, c):
    j = pl.program_id(2)
    cp = max(c, SUBLANES) if c < GDN_CP else GDN_CP
    cp = 1 << (cp - 1).bit_length()

    @pl.when(j == 0)
    def _():
        s_ref[...] = s0_ref[...]

    def conv(xp_ref, x_in, buf, cw, nb_ref):
        @pl.when(j == 0)
        def _():
            xp_ref[:, SUBLANES - (CONV_W - 1):SUBLANES, :] = buf[...]

        @pl.when(j > 0)
        def _():
            xp_ref[:, SUBLANES - (CONV_W - 1):SUBLANES, :] = xp_ref[:, c + SUBLANES - (CONV_W - 1):c + SUBLANES, :]

        xp_ref[:, SUBLANES:SUBLANES + c, :] = x_in[...]

        @pl.when(j == pl.num_programs(2) - 1)
        def _():
            nb_ref[...] = xp_ref[:, c + SUBLANES - (CONV_W - 1):c + SUBLANES, :]

        return _silu(_conv_silu_free(xp_ref, cw, None, c)[0])

    nh = q_ref.shape[2] // GDN_HEAD
    q_all = conv(xpq_ref, q_ref, bufq_ref, cwq_ref, nbq_ref)
    k_all = conv(xpk_ref, k_ref, bufk_ref, cwk_ref, nbk_ref)
    v_all = conv(xpv_ref, v_ref, bufv_ref, cwv_ref, nbv_ref)
    nv = nh * GDN_REP
    live = lax.broadcasted_iota(jnp.int32, (cp, LANES), 0) < c
    ti = lax.broadcasted_iota(jnp.int32, (1, cp, cp), 1)
    si = lax.broadcasted_iota(jnp.int32, (1, cp, cp), 2)
    sl = lambda h: slice(h * GDN_HEAD, (h + 1) * GDN_HEAD)
    qs, ks, vs, gcols, bcols, grows = [], [], [], [], [], []
    for hh in range(nh):
        q = _pad_rows(_l2norm(q_all[:, sl(hh)]) * (GDN_HEAD ** -0.5), cp)
        k = _pad_rows(_l2norm(k_all[:, sl(hh)]), cp)
        ba = _pad_rows(ba_ref[0, :, sl(hh)], cp)
        beta = jnp.where(live, _sigmoid(ba), 0.0)
        g = jnp.where(live, -jnp.exp(alog_ref[hh]) * _softplus(ba + dtb_ref[hh]), 0.0)
        gc = _cumsum_time(g)
        gc_t = _transpose_tile(gc)
        for i in range(GDN_REP):
            qs.append(q)
            ks.append(k)
            vs.append(_pad_rows(v_all[:, sl(hh * GDN_REP + i)], cp))
            gcols.append(gc[:, GDN_REP + i:GDN_REP + i + 1])
            bcols.append(beta[:, i:i + 1])
            grows.append(gc_t[GDN_REP + i:GDN_REP + i + 1, :])
    q, k, v = jnp.stack(qs), jnp.stack(ks), jnp.stack(vs)
    gcol, bcol, grow = jnp.stack(gcols), jnp.stack(bcols), jnp.stack(grows)
    glast = gcol[:, cp - 1:cp, :]
    dec_incl = jnp.exp(jnp.where(ti >= si, gcol - grow, -jnp.inf))
    a_mat = bcol * _dot_nt(k, k) * jnp.where(ti > si, dec_incl, 0.0)
    rhs = jnp.concatenate([v * bcol, k * (bcol * jnp.exp(gcol))], axis=-1)
    sol = _dot(_unit_lower_inverse(a_mat), rhs)
    u, w = sol[:, :, :GDN_HEAD], sol[:, :, GDN_HEAD:]
    s = s_ref[0]
    v_new = u - _dot(w, s)
    o = _dot(q * jnp.exp(gcol), s) + _dot(_dot_nt(q, k) * dec_incl, v_new)
    s_ref[0] = s * jnp.exp(glast) + _dot_tn(k * jnp.exp(glast - gcol), v_new)
    o = o[:, :c]
    o = o * lax.rsqrt(jnp.mean(o * o, axis=-1, keepdims=True) + NORM_EPS) * ng_ref[...]
    for vh in range(nv):
        o_ref[:, sl(vh)] = (o[vh] * _silu(z_ref[0, :, sl(vh)])).astype(o_ref.dtype)


def gdn_core(qkvz, ba, conv_buf, s0, conv_w, a_log, dt_bias, norm_g):
    bsz, seqlen, _ = qkvz.shape
    c = min(seqlen, GDN_CP)
    nc = seqlen // c
    kh = GDN_K_HEADS
    nh = GDN_HEADS_PER_STEP if nc > 1 else GDN_HEADS_PER_STEP_SHORT
    hd = nh * GDN_HEAD
    vw = GDN_REP * hd

    def head_pad(v):
        return jnp.pad(v.reshape(kh, 1, GDN_REP), ((0, 0), (0, 0), (GDN_REP, LANES - 2 * GDN_REP)))

    koff = GDN_KEY_DIM // hd
    voff = 2 * GDN_KEY_DIM // vw
    zoff = GDN_CONV_DIM // vw
    seq3 = lambda w, off: pl.BlockSpec((1, c, w), lambda b, h, j: (b, j, h + off))
    buf3 = lambda w, off: pl.BlockSpec((1, CONV_W - 1, w), lambda b, h, j: (b, 0, h + off))
    par2 = lambda w, off: pl.BlockSpec((CONV_W, w), lambda b, h, j: (0, h + off))
    hp = pl.BlockSpec((nh, 1, LANES), lambda b, h, j: (h, 0, 0))
    sspec = pl.BlockSpec((1, nh * GDN_REP, GDN_HEAD, GDN_HEAD), lambda b, h, j: (b, h, 0, 0))
    o, nbq, nbk, nbv, sfin = pl.pallas_call(
        functools.partial(_gdn_kernel, c=c),
        grid=(bsz, kh // nh, nc),
        in_specs=[seq3(hd, 0), seq3(hd, koff), seq3(vw, voff), seq3(vw, zoff), seq3(hd, 0),
                  buf3(hd, 0), buf3(hd, koff), buf3(vw, voff), sspec,
                  par2(hd, 0), par2(hd, koff), par2(vw, voff), hp, hp,
                  pl.BlockSpec((1, GDN_HEAD), lambda b, h, j: (0, 0))],
        out_specs=[pl.BlockSpec((c, vw), lambda b, h, j: (b * nc + j, h)),
                   buf3(hd, 0), buf3(hd, 0), buf3(vw, 0), sspec],
        out_shape=[jax.ShapeDtypeStruct((bsz * seqlen, GDN_VAL_DIM), F32),
                   jax.ShapeDtypeStruct((bsz, CONV_W - 1, GDN_KEY_DIM), F32),
                   jax.ShapeDtypeStruct((bsz, CONV_W - 1, GDN_KEY_DIM), F32),
                   jax.ShapeDtypeStruct((bsz, CONV_W - 1, GDN_VAL_DIM), F32),
                   jax.ShapeDtypeStruct(s0.shape, F32)],
        scratch_shapes=[pltpu.VMEM((1, c + SUBLANES, hd), F32), pltpu.VMEM((1, c + SUBLANES, hd), F32),
                        pltpu.VMEM((1, c + SUBLANES, vw), F32)],
        compiler_params=_cparams("parallel", "parallel", "arbitrary"),
        name="gdn_core",
    )(qkvz, qkvz, qkvz, qkvz, ba, conv_buf, conv_buf, conv_buf, s0,
      conv_w, conv_w, conv_w, head_pad(a_log), head_pad(dt_bias), norm_g.reshape(1, GDN_HEAD))
    return o, jnp.concatenate([nbq, nbk, nbv], axis=-1), sfin


def gdn_ba_weight(w_in):
    tail = w_in[:, GDN_CONV_DIM + GDN_VAL_DIM:]
    b_w = tail[:, :GDN_V_HEADS].reshape(-1, GDN_K_HEADS, GDN_REP)
    a_w = tail[:, GDN_V_HEADS:].reshape(-1, GDN_K_HEADS, GDN_REP)
    blk = jnp.concatenate([b_w, a_w], axis=-1)
    return jnp.pad(blk, ((0, 0), (0, 0), (0, LANES - 2 * GDN_REP))).reshape(-1, GDN_K_HEADS * LANES)


N_EXPERTS = 32
TOP_K = 4
D_FF = 2048
SWIGLU_LIMIT = 7.0
SWIGLU_ALPHA = 1.702
MOE_ROWS = 512
MOE_ROUTER_TOKENS = 512
MOE_TN_UP = 1024
MOE_TN_DOWN = 2048
MOE_TOKENS_PER_STEP = 256


def _router_kernel(x_ref, w_ref, b_ref, e_ref, g_ref, r_ref, cnt_ref, xp_ref, carry_ref):
    @pl.when(pl.program_id(0) == 0)
    def _():
        carry_ref[...] = jnp.zeros(carry_ref.shape, F32)

    xp_ref[...] = _pack_rows(x_ref[...])
    logits = _dot6(x_ref[...], w_ref[...]) + b_ref[...]
    tm = logits.shape[0]
    lane = lax.broadcasted_iota(jnp.int32, logits.shape, 1)
    logits = jnp.where(lane < N_EXPERTS, logits, -jnp.inf)
    e_out = jnp.zeros(logits.shape, jnp.int32)
    g_out = jnp.zeros(logits.shape, F32)
    picked = jnp.zeros(logits.shape, F32)
    top, idxs = None, []
    for kth in range(TOP_K):
        m = jnp.max(logits, axis=-1, keepdims=True)
        idx = jnp.min(jnp.where(logits == m, lane, LANES), axis=-1, keepdims=True)
        top = m if top is None else top
        idxs.append(idx)
        e_out = jnp.where(lane == kth, idx, e_out)
        g_out = jnp.where(lane == kth, jnp.exp(m - top), g_out)
        picked = jnp.where(lane == idx, 1.0, picked)
        logits = jnp.where(lane == idx, -jnp.inf, logits)
    e_ref[0] = e_out.T[:SUBLANES]
    g_ref[...] = g_out / jnp.sum(g_out, axis=-1, keepdims=True)
    earlier = (lax.broadcasted_iota(jnp.int32, (tm, tm), 0) > lax.broadcasted_iota(jnp.int32, (tm, tm), 1))
    before = carry_ref[...] + _dot(earlier.astype(F32), picked)
    r_out = jnp.zeros(logits.shape, jnp.int32)
    for kth in range(TOP_K):
        rk = jnp.sum(jnp.where(lane == idxs[kth], before, 0.0), axis=-1, keepdims=True)
        r_out = jnp.where(lane == kth, rk.astype(jnp.int32), r_out)
    r_ref[0] = r_out.T[:SUBLANES]
    carry_ref[...] = carry_ref[...] + jnp.sum(picked, axis=0, keepdims=True)
    cnt_ref[...] = carry_ref[...].astype(jnp.int32)


def moe_router(x, w_router, b_router, layer):
    n_tok, d = x.shape
    tm = MOE_ROUTER_TOKENS
    assert n_tok % tm == 0
    w = jnp.pad(w_router[layer], ((0, 0), (0, LANES - N_EXPERTS)))
    b = jnp.pad(b_router[layer], (0, LANES - N_EXPERTS)).reshape(1, LANES)
    tok = pl.BlockSpec((tm, LANES), lambda i: (i, 0))
    tile = pl.BlockSpec((1, SUBLANES, tm), lambda i: (i, 0, 0))
    return pl.pallas_call(
        _router_kernel,
        grid=(n_tok // tm,),
        in_specs=[pl.BlockSpec((tm, d), lambda i: (i, 0)), pl.BlockSpec((d, LANES), lambda i: (0, 0)),
                  pl.BlockSpec((1, LANES), lambda i: (0, 0))],
        out_specs=[tile, tok, tile, pl.BlockSpec((1, LANES), lambda i: (0, 0)),
                   pl.BlockSpec((tm, d // 2), lambda i: (i, 0))],
        out_shape=[jax.ShapeDtypeStruct((n_tok // tm, SUBLANES, tm), jnp.int32),
                   jax.ShapeDtypeStruct((n_tok, LANES), F32),
                   jax.ShapeDtypeStruct((n_tok // tm, SUBLANES, tm), jnp.int32),
                   jax.ShapeDtypeStruct((1, LANES), jnp.int32),
                   jax.ShapeDtypeStruct((n_tok, d // 2), jnp.uint32)],
        scratch_shapes=[pltpu.VMEM((1, LANES), F32)],
        compiler_params=_cparams("arbitrary"),
        name="moe_router",
    )(x, w, b)


def moe_blocks(counts, n_pairs):
    nb_e = (counts + MOE_ROWS - 1) // MOE_ROWS
    bend = jnp.cumsum(nb_e)
    return nb_e, bend - nb_e, bend, n_pairs // MOE_ROWS + N_EXPERTS


def _issue_and_drain(n_tok, start_one, wait_one):
    def start(t, carry):
        for kth in range(TOP_K):
            start_one(t, kth)
        return carry

    def wait(t, carry):
        for _ in range(TOP_K):
            wait_one()
        return carry

    lax.fori_loop(0, n_tok, start, 0)
    lax.fori_loop(0, n_tok, wait, 0)


def _pair_tile_spec(e_t, tm):
    per = e_t.shape[2] // tm
    return pl.BlockSpec((1, SUBLANES, tm), lambda i, *_: (i // per, 0, i % per), memory_space=pltpu.SMEM)


def _dispatch_kernel(bstart_ref, nb_ref, cnt_ref, slot_ref, x_ref, xs_ref, zero_ref, sem):
    tm = x_ref.shape[0]

    @pl.when(pl.program_id(0) == 0)
    def _():
        zero_ref[...] = jnp.zeros(zero_ref.shape, zero_ref.dtype)

        def zero_row(row):
            return pltpu.make_async_copy(zero_ref.at[pl.ds(0, 1)], xs_ref.at[pl.ds(row, 1)], sem)

        def zero_piece(row):
            return pltpu.make_async_copy(zero_ref, xs_ref.at[pl.ds(pl.multiple_of(row, SUBLANES), SUBLANES)], sem)

        def drain(n, copy):
            def wait(c, cc):
                copy.wait()
                return cc

            lax.fori_loop(0, n, wait, 0)

        def fill(e, carry):
            first = bstart_ref[e] * MOE_ROWS + cnt_ref[e]
            singles = (-cnt_ref[e]) % SUBLANES
            pieces = (nb_ref[e] * MOE_ROWS - cnt_ref[e] - singles) // SUBLANES

            def start_row(c, cc):
                zero_row(first + c).start()
                return cc

            def start_piece(c, cc):
                zero_piece(first + singles + c * SUBLANES).start()
                return cc

            lax.fori_loop(0, singles, start_row, 0)
            lax.fori_loop(0, pieces, start_piece, 0)
            drain(singles, zero_row(0))
            drain(pieces, zero_piece(0))
            return carry

        lax.fori_loop(0, N_EXPERTS + 1, fill, 0)

    def row_copy(src_row, slot):
        return pltpu.make_async_copy(x_ref.at[pl.ds(src_row, 1)], xs_ref.at[pl.ds(slot, 1)], sem)

    def start_one(t, kth):
        row_copy(t, slot_ref[0, kth, t]).start()

    _issue_and_drain(tm, start_one, lambda: row_copy(0, 0).wait())


def moe_pair_slots(e_t, r_t, bstart):
    first = jnp.zeros(e_t.shape, jnp.int32)
    for e in range(N_EXPERTS):
        first = jnp.where(e_t == e, bstart[e], first)
    return first * MOE_ROWS + r_t


def moe_dispatch(x_rows, slot_t, bstart, nb_e, counts, n_blocks):
    n_tok, width = x_rows.shape
    tm = MOE_TOKENS_PER_STEP
    assert n_tok % tm == 0
    smem = _pair_tile_spec(slot_t, tm)
    used = bstart[-1:] + nb_e[-1:]
    bstart = jnp.concatenate([bstart, used])
    nb_e = jnp.concatenate([nb_e, n_blocks - used])
    counts = jnp.concatenate([counts, jnp.zeros((1,), counts.dtype)])
    return pl.pallas_call(
        _dispatch_kernel,
        grid_spec=pltpu.PrefetchScalarGridSpec(
            num_scalar_prefetch=3, grid=(n_tok // tm,),
            in_specs=[smem, pl.BlockSpec((tm, width), lambda i, *_: (i, 0))],
            out_specs=pl.BlockSpec(memory_space=pl.ANY),
            scratch_shapes=[pltpu.VMEM((SUBLANES, width), x_rows.dtype), pltpu.SemaphoreType.DMA(())]),
        out_shape=jax.ShapeDtypeStruct((n_blocks * MOE_ROWS, width), x_rows.dtype),
        compiler_params=_cparams("arbitrary"),
        name="moe_dispatch",
    )(bstart, nb_e, counts, slot_t, x_rows)


def _combine_kernel(slot_ref, g_ref, ys_ref, o_ref, buf_ref, sem):
    tm = o_ref.shape[0]

    def row_copy(slot, dst_row):
        return pltpu.make_async_copy(ys_ref.at[pl.ds(slot, 1)], buf_ref.at[pl.ds(dst_row, 1)], sem)

    def start_one(t, kth):
        row_copy(slot_ref[0, kth, t], kth * tm + t).start()

    _issue_and_drain(tm, start_one, lambda: row_copy(0, 0).wait())
    acc = None
    for kth in range(TOP_K):
        term = g_ref[:, kth:kth + 1] * buf_ref[kth * tm:(kth + 1) * tm, :]
        acc = term if acc is None else acc + term
    o_ref[...] = acc


def moe_combine(ys, slot_t, gates):
    d = ys.shape[1]
    n_tok = gates.shape[0]
    tm = MOE_TOKENS_PER_STEP
    return pl.pallas_call(
        _combine_kernel,
        grid=(n_tok // tm,),
        in_specs=[_pair_tile_spec(slot_t, tm), pl.BlockSpec((tm, LANES), lambda i: (i, 0)),
                  pl.BlockSpec(memory_space=pl.ANY)],
        out_specs=pl.BlockSpec((tm, d), lambda i: (i, 0)),
        out_shape=jax.ShapeDtypeStruct((n_tok, d), F32),
        scratch_shapes=[pltpu.VMEM((TOP_K * tm, d), F32), pltpu.SemaphoreType.DMA(())],
        compiler_params=_cparams("arbitrary"),
        name="moe_combine",
    )(slot_t, gates, ys)


def moe_items(nb_e, bstart, bend, n_blocks, nt):
    i = jnp.arange(n_blocks * nt, dtype=jnp.int32)
    used = bend[-1]
    e_i = jnp.minimum(jnp.sum((i[:, None] >= (bend * nt)[None, :]).astype(jnp.int32), axis=1), N_EXPERTS - 1)
    local = i - bstart[e_i] * nt
    nbe = jnp.maximum(nb_e[e_i], 1)
    valid = i < used * nt
    spare = i - used * nt
    blk = jnp.where(valid, bstart[e_i] + local % nbe, used + spare // nt)
    blk_in = jnp.where(valid, blk, 0)
    n_out = jnp.where(valid, local // nbe, spare % nt)
    e_last = jnp.max(jnp.where(nb_e > 0, jnp.arange(N_EXPERTS, dtype=jnp.int32), 0))
    e_w = jnp.where(valid, e_i, e_last)
    n_w = jnp.where(valid, local // nbe, nt - 1)
    first = (valid & (local % nbe == 0)).astype(jnp.int32)
    i32 = lambda a: a.astype(jnp.int32)
    return i32(blk_in), i32(blk), i32(e_w), i32(n_w), i32(n_out), first, i32(valid)


def _unpack_rows(words):
    lo = lax.bitcast_convert_type(words << 16, F32)
    hi = lax.bitcast_convert_type(words & jnp.uint32(0xFFFF0000), F32)
    return jnp.concatenate([lo, hi], axis=-1).astype(BF16)


def _pack_rows(x):
    half = x.shape[-1] // 2
    bits = lambda v: lax.bitcast_convert_type(v.astype(BF16).astype(F32), jnp.uint32)
    return (bits(x[:, half:]) & jnp.uint32(0xFFFF0000)) | (bits(x[:, :half]) >> 16)


def _gmm_kernel(bin_ref, blk_ref, ew_ref, nw_ref, no_ref, first_ref, valid_ref, x_ref, *rest, glu):
    i = pl.program_id(0)
    if glu:
        wg_ref, wu_ref, bg_ref, bu_ref, o_ref, wgb_ref, wub_ref = rest
    else:
        w_ref, b_ref, o_ref, wb_ref = rest

    @pl.when(first_ref[i] == 1)
    def _():
        if glu:
            wgb_ref[...] = wg_ref[...].astype(BF16)
            wub_ref[...] = wu_ref[...].astype(BF16)
        else:
            wb_ref[...] = w_ref[...].astype(BF16)

    @pl.when(valid_ref[i] == 1)
    def _():
        if glu:
            x = _unpack_rows(x_ref[...])
            gate = jnp.dot(x, wgb_ref[...], preferred_element_type=F32) + bg_ref[...]
            up = jnp.dot(x, wub_ref[...], preferred_element_type=F32) + bu_ref[...]
            gate = jnp.minimum(gate, SWIGLU_LIMIT)
            up = jnp.clip(up, -SWIGLU_LIMIT, SWIGLU_LIMIT)
            out = (up + 1.0) * (gate * _sigmoid(SWIGLU_ALPHA * gate))
        else:
            out = jnp.dot(x_ref[...], wb_ref[...], preferred_element_type=F32) + b_ref[...]
        o_ref[...] = out.astype(o_ref.dtype)

    @pl.when(valid_ref[i] == 0)
    def _():
        o_ref[...] = jnp.zeros(o_ref.shape, o_ref.dtype)


def grouped_matmul(xs, w, b, layer, items, *, glu):
    k = w.shape[-2]
    n_blocks = xs.shape[0] // MOE_ROWS
    n_out = w.shape[-1] // 2 if glu else w.shape[-1]
    tn = MOE_TN_UP if glu else MOE_TN_DOWN
    n_items = items[0].shape[0]
    b4 = b.reshape(b.shape[0], b.shape[1], 1, b.shape[2])
    wspec = lambda off: pl.BlockSpec((None, None, k, tn),
                                     lambda i, bi, blk, ew, nw, no, fi, va: (layer, ew[i], 0, nw[i] + off))
    bspec = lambda off: pl.BlockSpec((None, None, 1, tn),
                                     lambda i, bi, blk, ew, nw, no, fi, va: (layer, ew[i], 0, nw[i] + off))
    xspec = pl.BlockSpec((MOE_ROWS, xs.shape[1]), lambda i, bi, blk, ew, nw, no, fi, va: (bi[i], 0))
    ospec = pl.BlockSpec((MOE_ROWS, tn), lambda i, bi, blk, ew, nw, no, fi, va: (blk[i], no[i]))
    if glu:
        half = n_out // tn
        in_specs = [xspec, wspec(0), wspec(half), bspec(0), bspec(half)]
        args = (xs, w, w, b4, b4)
        scratch = [pltpu.VMEM((k, tn), BF16)] * 2
    else:
        in_specs = [xspec, wspec(0), bspec(0)]
        args = (xs, w, b4)
        scratch = [pltpu.VMEM((k, tn), BF16)]
    return pl.pallas_call(
        functools.partial(_gmm_kernel, glu=glu),
        grid_spec=pltpu.PrefetchScalarGridSpec(
            num_scalar_prefetch=7, grid=(n_items,), in_specs=in_specs, out_specs=ospec, scratch_shapes=scratch),
        out_shape=jax.ShapeDtypeStruct((n_blocks * MOE_ROWS, n_out), BF16 if glu else F32),
        compiler_params=_cparams("arbitrary"),
        name="moe_up" if glu else "moe_down",
    )(*items, *args)


def moe_ffn(hn, layer, w_router, b_router, w_gu, b_gu, w_down, b_down):
    n_tok, d = hn.shape
    e_t, gates, r_t, counts, x_rows = moe_router(hn, w_router, b_router, layer)
    counts = counts[0, :N_EXPERTS]
    nb_e, bstart, bend, n_blocks = moe_blocks(counts, n_tok * TOP_K)
    slot_t = moe_pair_slots(e_t, r_t, bstart)
    xs = moe_dispatch(x_rows, slot_t, bstart, nb_e, counts, n_blocks)
    hmid = grouped_matmul(xs, w_gu, b_gu, layer, moe_items(nb_e, bstart, bend, n_blocks, D_FF // MOE_TN_UP), glu=True)
    ys = grouped_matmul(hmid, w_down, b_down, layer,
                        moe_items(nb_e, bstart, bend, n_blocks, D_MODEL // MOE_TN_DOWN), glu=False)
    return moe_combine(ys, slot_t, gates)


def ssd_mixer(hn, conv_buf, s0, w_in, conv_w, conv_b, dt_bias, a_log, d_skip, norm_g, w_out):
    bsz, seqlen, d = hn.shape
    h2 = hn.reshape(bsz * seqlen, d)
    zxbc = matmul(h2, w_in, w_index=(0,), n_cols=SSD_D_INNER + SSD_CONV_DIM).reshape(bsz, seqlen, -1)
    dt_raw = matmul(h2, ssd_dt_weight(w_in[0])).reshape(bsz, seqlen, -1)
    y, new_buf, s_fin = ssd_core(zxbc, dt_raw, conv_buf, s0, conv_w[0], conv_b[0], dt_bias[0], a_log[0], d_skip[0],
                                 norm_g[0])
    return matmul(y, w_out, w_index=(0,)), new_buf, s_fin


def lru_mixer(hn, conv_buf, h0, w_in, conv_w, conv_b, w_rg, b_rg, w_ig, b_ig, lam, w_out):
    bsz, seqlen, d = hn.shape
    xg = matmul(hn.reshape(bsz * seqlen, d), w_in, w_index=(0,)).reshape(bsz, seqlen, -1)
    y, new_buf, h_fin = lru_core(xg, conv_buf, h0, conv_w[0], conv_b[0], w_rg[0], b_rg[0].reshape(-1), w_ig[0],
                                 b_ig[0].reshape(-1), lam[0])
    return matmul(y, w_out, w_index=(0,)), new_buf, h_fin


def gdn_mixer(hn, conv_buf, s0, w_in, conv_w, a_log, dt_bias, norm_g, w_out):
    bsz, seqlen, d = hn.shape
    h2 = hn.reshape(bsz * seqlen, d)
    qkvz = matmul(h2, w_in, w_index=(0,), n_cols=GDN_CONV_DIM + GDN_VAL_DIM).reshape(bsz, seqlen, -1)
    ba = matmul(h2, gdn_ba_weight(w_in[0])).reshape(bsz, seqlen, -1)
    o, new_buf, s_fin = gdn_core(qkvz, ba, conv_buf, s0, conv_w[0], a_log[0], dt_bias[0], norm_g[0])
    return matmul(o, w_out, w_index=(0,)), new_buf, s_fin


def kernel(x_prompt, x_sample, c_prompt, c_sample, state_ssd, cache_ssd_conv, state_rwkv, cache_rwkv_shift, state_lru, cache_lru_conv, state_gdn, cache_gdn_conv, w_ada, b_ada, norm_g, final_g, ssd_w_in, ssd_conv_w, ssd_conv_b, ssd_dt_bias, ssd_a_log, ssd_d, ssd_norm_g, ssd_w_out, rwkv_mu, rwkv_w_rkv, rwkv_w0, rwkv_w1, rwkv_w2, rwkv_a0, rwkv_a1, rwkv_a2, rwkv_g1, rwkv_g2, rwkv_k_k, rwkv_k_a, rwkv_r_k, rwkv_lnx_g, rwkv_lnx_b, rwkv_w_o, lru_w_in, lru_conv_w, lru_conv_b, lru_w_rg, lru_b_rg, lru_w_ig, lru_b_ig, lru_lambda, lru_w_out, gdn_w_in, gdn_conv_w, gdn_a_log, gdn_dt_bias, gdn_norm_g, gdn_w_out, moe_w_router, moe_b_router, moe_w_gu, moe_b_gu, moe_w_down, moe_b_down):
    assert DEPTH == 4 and w_ada.shape[0] == DEPTH
    d = D_MODEL
    xs = [x_prompt, x_sample]
    n_seq = [x.shape[0] for x in xs]
    n_rows = [x.shape[0] * x.shape[1] for x in xs]
    row0 = [0, n_rows[0]]

    cond = jnp.concatenate([c_prompt, c_sample], axis=0)
    pad = (-cond.shape[0]) % (2 * SUBLANES)
    cond = jnp.pad(cond, ((0, pad), (0, 0)))
    mods = []
    for i in range(DEPTH):
        m = matmul(cond, w_ada, b_ada, w_index=(i,), pre_silu=True, tn=ADA_TN)
        mods.append([m[:n_seq[0]].reshape(n_seq[0], 1, 6 * d),
                     m[n_seq[0]:n_seq[0] + n_seq[1]].reshape(n_seq[1], 1, 6 * d)])

    def zeros_like_state(s, bsz):
        return jnp.zeros((bsz,) + s.shape[2:], F32)

    st = {
        "ssd": [zeros_like_state(state_ssd, n_seq[0]), state_ssd[0]],
        "ssd_conv": [zeros_like_state(cache_ssd_conv, n_seq[0]), cache_ssd_conv[0]],
        "rwkv": [zeros_like_state(state_rwkv, n_seq[0]), state_rwkv[0]],
        "rwkv_shift": [zeros_like_state(cache_rwkv_shift, n_seq[0]), cache_rwkv_shift[0]],
        "lru": [zeros_like_state(state_lru, n_seq[0]), state_lru[0]],
        "lru_conv": [zeros_like_state(cache_lru_conv, n_seq[0]), cache_lru_conv[0]],
        "gdn": [zeros_like_state(state_gdn, n_seq[0]), state_gdn[0]],
        "gdn_conv": [zeros_like_state(cache_gdn_conv, n_seq[0]), cache_gdn_conv[0]],
    }
    new = {}

    hn = [norm_mod(xs[gi], norm_g[0, 0], mods[0][gi], 0) for gi in range(2)]
    outs = None
    for i in range(DEPTH):
        ys = []
        for gi in range(2):
            if i == 0:
                y, buf, s = ssd_mixer(hn[gi], st["ssd_conv"][gi], st["ssd"][gi], ssd_w_in, ssd_conv_w, ssd_conv_b,
                                      ssd_dt_bias, ssd_a_log, ssd_d, ssd_norm_g, ssd_w_out)
                new.setdefault("ssd", []).append(s)
                new.setdefault("ssd_conv", []).append(buf)
            elif i == 1:
                y, buf, s = rwkv_mixer(hn[gi], st["rwkv_shift"][gi], st["rwkv"][gi], rwkv_mu[0], rwkv_w_rkv[0],
                                       rwkv_w0[0], rwkv_w1[0], rwkv_w2[0], rwkv_a0[0], rwkv_a1[0], rwkv_a2[0],
                                       rwkv_g1[0], rwkv_g2[0], rwkv_k_k[0], rwkv_k_a[0], rwkv_r_k[0],
                                       rwkv_lnx_g[0], rwkv_lnx_b[0], rwkv_w_o[0])
                new.setdefault("rwkv", []).append(s)
                new.setdefault("rwkv_shift", []).append(buf)
            elif i == 2:
                y, buf, s = lru_mixer(hn[gi], st["lru_conv"][gi], st["lru"][gi], lru_w_in, lru_conv_w, lru_conv_b,
                                      lru_w_rg, lru_b_rg, lru_w_ig, lru_b_ig, lru_lambda, lru_w_out)
                new.setdefault("lru", []).append(s)
                new.setdefault("lru_conv", []).append(buf)
            else:
                y, buf, s = gdn_mixer(hn[gi], st["gdn_conv"][gi], st["gdn"][gi], gdn_w_in, gdn_conv_w, gdn_a_log,
                                      gdn_dt_bias, gdn_norm_g, gdn_w_out)
                new.setdefault("gdn", []).append(s)
                new.setdefault("gdn_conv", []).append(buf)
            ys.append(y)
        hn2 = []
        for gi in range(2):
            xs[gi], h2 = resid_norm(xs[gi], ys[gi], 0, norm_g[i, 1], mods[i][gi], 0, mods[i][gi], 1)
            hn2.append(h2.reshape(n_rows[gi], d))
        moe = moe_ffn(jnp.concatenate(hn2, axis=0), i, moe_w_router, moe_b_router, moe_w_gu, moe_b_gu, moe_w_down,
                      moe_b_down)
        if i + 1 < DEPTH:
            for gi in range(2):
                xs[gi], hn[gi] = resid_norm(xs[gi], moe, row0[gi], norm_g[i + 1, 0], mods[i][gi], 1,
                                            mods[i + 1][gi], 0)
        else:
            outs = [resid_final(xs[gi], moe, row0[gi], final_g, mods[i][gi], 1) for gi in range(2)]

    res = [outs[0], outs[1]]
    for key in ("ssd", "ssd_conv", "rwkv", "rwkv_shift", "lru", "lru_conv", "gdn", "gdn_conv"):
        res += [new[key][0][None], new[key][1][None]]
    return tuple(res)
```

```python
import functools
import math

import jax
import jax.numpy as jnp
from jax import lax
from jax.experimental import pallas as pl
from jax.experimental.pallas import tpu as pltpu

F32 = jnp.float32
BF16 = jnp.bfloat16

VMEM_LIMIT_BYTES = 56 * 1024 * 1024
LANES = 128
SUBLANES = 8

D_MODEL = 2048
DEPTH = 4
CONV_W = 4
NORM_EPS = 1e-6
L2_EPS = 1e-6

LRU_BLOCKS = 8
LRU_C = 8.0
ADA_TN = 2048


def _cparams(*sem):
    return pltpu.CompilerParams(dimension_semantics=sem, vmem_limit_bytes=VMEM_LIMIT_BYTES)


def _sigmoid(x):
    return 1.0 / (1.0 + jnp.exp(-x))


def _silu(x):
    return x * _sigmoid(x)


def _softplus(x):
    return jnp.maximum(x, 0.0) + jnp.log(1.0 + jnp.exp(-jnp.abs(x)))


def _expm1(x):
    u = jnp.exp(x)
    safe = jnp.where(u == 1.0, 2.0, u)
    return jnp.where(u == 1.0, x, (u - 1.0) * x / jnp.log(safe))


def _dims(a, ca, cb):
    lead = a.ndim - 2
    batch = tuple(range(lead))
    return (((ca + lead,), (cb + lead,)), (batch, batch))


def _dot(a, b):
    return lax.dot_general(a.astype(BF16), b.astype(BF16), _dims(a, 1, 0), preferred_element_type=F32)


def _dot_nt(a, b):
    return lax.dot_general(a.astype(BF16), b.astype(BF16), _dims(a, 1, 1), preferred_element_type=F32)


def _dot_tn(a, b):
    return lax.dot_general(a.astype(BF16), b.astype(BF16), _dims(a, 0, 0), preferred_element_type=F32)


def _split3(a):
    hi = a.astype(BF16)
    r1 = a - hi.astype(F32)
    mid = r1.astype(BF16)
    lo = (r1 - mid.astype(F32)).astype(BF16)
    return hi, mid, lo


def _dot3(a, b):
    ah, am, _ = _split3(a)
    bh, bm, _ = _split3(b)
    return (_dot(ah, bm) + _dot(am, bh)) + _dot(ah, bh)


def _dot6(a, b):
    ah, am, al = _split3(a)
    bh, bm, bl = _split3(b)
    dot = lambda x, y: jnp.dot(x, y, preferred_element_type=F32)
    small = dot(am, bm) + dot(ah, bl) + dot(al, bh)
    return (small + dot(ah, bm) + dot(am, bh)) + dot(ah, bh)


def _mm_kernel(x_ref, w_ref, *rest, has_bias, pre_silu):
    if has_bias:
        b_ref, o_ref, xb_ref = rest
    else:
        o_ref, xb_ref = rest

    @pl.when(pl.program_id(1) == 0)
    def _():
        x = x_ref[...]
        xb_ref[...] = (_silu(x) if pre_silu else x).astype(BF16)

    acc = jnp.dot(xb_ref[...], w_ref[...].astype(BF16), preferred_element_type=F32)
    if has_bias:
        acc = acc + b_ref[...]
    o_ref[...] = acc.astype(o_ref.dtype)


def _pick_tile(n, pref):
    t = min(n, pref)
    while n % t:
        t //= 2
    return t


def matmul(x, w, b=None, *, w_index=(), n_cols=None, col0=0, out_dtype=F32, tm=None, tn=None, pre_silu=False):
    m, k = x.shape
    lead = len(w_index)
    assert w.shape[lead] == k
    n_total = w.shape[lead + 1]
    n = n_total if n_cols is None else n_cols
    if tm is None:
        tm = _pick_tile(m, 1024 if k <= 2048 else 512)
    if tn is None:
        tn = _pick_tile(n, 512)
    assert m % tm == 0 and n % tn == 0 and col0 % tn == 0
    cb = col0 // tn
    widx = tuple(w_index)
    in_specs = [
        pl.BlockSpec((tm, k), lambda i, j: (i, 0)),
        pl.BlockSpec((None,) * lead + (k, tn), lambda i, j: widx + (0, j + cb)),
    ]
    args = [x, w]
    if b is not None:
        bb = b.reshape(b.shape[:lead] + (1, n_total))
        in_specs.append(pl.BlockSpec((None,) * lead + (1, tn), lambda i, j: widx + (0, j + cb)))
        args.append(bb)
    return pl.pallas_call(
        functools.partial(_mm_kernel, has_bias=b is not None, pre_silu=pre_silu),
        grid=(m // tm, n // tn),
        in_specs=in_specs,
        out_specs=pl.BlockSpec((tm, tn), lambda i, j: (i, j)),
        out_shape=jax.ShapeDtypeStruct((m, n), out_dtype),
        scratch_shapes=[pltpu.VMEM((tm, k), BF16)],
        compiler_params=_cparams("parallel", "arbitrary"),
        name="matmul",
    )(*args)


def _rms(x, g):
    return x * lax.rsqrt(jnp.mean(x * x, axis=-1, keepdims=True) + NORM_EPS) * g


def _norm_mod_kernel(x_ref, g_ref, sh_ref, sc_ref, o_ref):
    y = _rms(x_ref[...], g_ref[...])
    o_ref[...] = y * (1.0 + sc_ref[...]) + sh_ref[...]


def _seq_blocks(bsz, seqlen):
    if seqlen >= 256:
        return 1, 256
    return max(1, 128 // seqlen), seqlen


def _mod_spec(bb, col):
    return pl.BlockSpec((bb, 1, D_MODEL), lambda i, j: (i, 0, col))


def norm_mod(x, g, mod, which):
    bsz, seqlen, d = x.shape
    bb, bl = _seq_blocks(bsz, seqlen)
    xspec = pl.BlockSpec((bb, bl, d), lambda i, j: (i, j, 0))
    return pl.pallas_call(
        _norm_mod_kernel,
        grid=(bsz // bb, seqlen // bl),
        in_specs=[xspec, pl.BlockSpec((1, 1, d), lambda i, j: (0, 0, 0)),
                  _mod_spec(bb, 3 * which), _mod_spec(bb, 3 * which + 1)],
        out_specs=xspec,
        out_shape=jax.ShapeDtypeStruct(x.shape, F32),
        compiler_params=_cparams("parallel", "parallel"),
        name="norm_mod",
    )(x, g.reshape(1, 1, d), mod, mod)


def _resid_norm_kernel(x_ref, y_ref, gt_ref, g_ref, sh_ref, sc_ref, xo_ref, ho_ref):
    xn = x_ref[...] + gt_ref[...] * y_ref[...]
    xo_ref[...] = xn
    ho_ref[...] = _rms(xn, g_ref[...]) * (1.0 + sc_ref[...]) + sh_ref[...]


def _rows_view(y, row0, bsz, seqlen, bb, bl):
    d = y.shape[-1]
    assert bb == 1 or bl == seqlen
    assert row0 % (bb * bl) == 0
    base, per_seq = row0 // (bb * bl), seqlen // bl
    return y.reshape(-1, bl, d), pl.BlockSpec((bb, bl, d), lambda i, j: (base + i * per_seq + j, 0, 0))


def resid_norm(x, y, row0, g, mod_gate, which_gate, mod_norm, which_norm):
    bsz, seqlen, d = x.shape
    bb, bl = _seq_blocks(bsz, seqlen)
    xspec = pl.BlockSpec((bb, bl, d), lambda i, j: (i, j, 0))
    y3, yspec = _rows_view(y, row0, bsz, seqlen, bb, bl)
    return pl.pallas_call(
        _resid_norm_kernel,
        grid=(bsz // bb, seqlen // bl),
        in_specs=[xspec, yspec, _mod_spec(bb, 3 * which_gate + 2),
                  pl.BlockSpec((1, 1, d), lambda i, j: (0, 0, 0)),
                  _mod_spec(bb, 3 * which_norm), _mod_spec(bb, 3 * which_norm + 1)],
        out_specs=[xspec, xspec],
        out_shape=[jax.ShapeDtypeStruct(x.shape, F32)] * 2,
        compiler_params=_cparams("parallel", "parallel"),
        name="resid_norm",
    )(x, y3, mod_gate, g.reshape(1, 1, d), mod_norm, mod_norm)


def _resid_final_kernel(x_ref, y_ref, gt_ref, g_ref, o_ref):
    o_ref[...] = _rms(x_ref[...] + gt_ref[...] * y_ref[...], g_ref[...])


def resid_final(x, y, row0, g, mod_gate, which_gate):
    bsz, seqlen, d = x.shape
    bb, bl = _seq_blocks(bsz, seqlen)
    xspec = pl.BlockSpec((bb, bl, d), lambda i, j: (i, j, 0))
    y3, yspec = _rows_view(y, row0, bsz, seqlen, bb, bl)
    return pl.pallas_call(
        _resid_final_kernel,
        grid=(bsz // bb, seqlen // bl),
        in_specs=[xspec, yspec, _mod_spec(bb, 3 * which_gate + 2),
                  pl.BlockSpec((1, 1, d), lambda i, j: (0, 0, 0))],
        out_specs=xspec,
        out_shape=jax.ShapeDtypeStruct(x.shape, F32),
        compiler_params=_cparams("parallel", "parallel"),
        name="resid_final",
    )(x, y3, mod_gate, g.reshape(1, 1, d))


def _shift_time(x, d, fill, axis):
    t = lax.broadcasted_iota(jnp.int32, x.shape, axis)
    return jnp.where(t >= d, pltpu.roll(x, d, axis), fill)


def _conv_silu_free(xp_ref, w_ref, b, tlen):
    acc = None
    for k in range(CONV_W):
        term = xp_ref[:, pl.ds(SUBLANES - (CONV_W - 1) + k, tlen), :] * w_ref[k:k + 1, :]
        acc = term if acc is None else acc + term
    if b is not None:
        acc = acc + b
    return acc


def _load_history(xp_ref, x_ref, buf_ref, tlen):
    j = pl.program_id(1)

    @pl.when(j == 0)
    def _():
        xp_ref[:, SUBLANES - (CONV_W - 1):SUBLANES, :] = buf_ref[...]

    @pl.when(j > 0)
    def _():
        xp_ref[:, SUBLANES - (CONV_W - 1):SUBLANES, :] = xp_ref[:, tlen + SUBLANES - (CONV_W - 1):tlen + SUBLANES, :]

    xp_ref[:, SUBLANES:SUBLANES + tlen, :] = x_ref[...]


def _gelu_tanh(x):
    return 0.5 * x * (1.0 + jnp.tanh(math.sqrt(2.0 / math.pi) * (x + 0.044715 * (x * x * x))))


def _lru_kernel(xb_ref, gb_ref, buf_ref, h0_ref, cw_ref, cb_ref, wrg_ref, brg_ref, wig_ref, big_ref, lam_ref,
                y_ref, nbuf_ref, hfin_ref, xp_ref, h_ref, *, tlen):
    j = pl.program_id(1)
    bb = xb_ref.shape[0]
    width = xb_ref.shape[2]
    blk = width // LRU_BLOCKS

    @pl.when(j == 0)
    def _():
        h_ref[...] = h0_ref[...]

    _load_history(xp_ref, xb_ref, buf_ref, tlen)
    xc = _conv_silu_free(xp_ref, cw_ref, cb_ref[...], tlen)
    x2 = xc.reshape(bb * tlen, width)
    gr, gi = [], []
    for h in range(LRU_BLOCKS):
        xh = x2[:, h * blk:(h + 1) * blk]
        gr.append(_dot(xh, wrg_ref[h]))
        gi.append(_dot(xh, wig_ref[h]))
    gate_r = _sigmoid(jnp.concatenate(gr, axis=-1) + brg_ref[...]).reshape(bb, tlen, width)
    gate_i = _sigmoid(jnp.concatenate(gi, axis=-1) + big_ref[...]).reshape(bb, tlen, width)
    log_a = (-LRU_C) * gate_r * _softplus(-lam_ref[...])
    a = jnp.exp(log_a)
    bx = jnp.sqrt(-_expm1(2.0 * log_a)) * gate_i * xc
    d = 1
    while d < tlen:
        bx = a * _shift_time(bx, d, 0.0, 1) + bx
        a = a * _shift_time(a, d, 1.0, 1)
        d *= 2
    hs = bx + a * h_ref[...]
    h_ref[...] = hs[:, tlen - 1:tlen, :]
    y = hs * _gelu_tanh(gb_ref[...])
    y_ref[...] = y.reshape(bb * tlen, width).astype(y_ref.dtype)

    @pl.when(j == pl.num_programs(1) - 1)
    def _():
        nbuf_ref[...] = xp_ref[:, tlen + SUBLANES - (CONV_W - 1):tlen + SUBLANES, :]
        hfin_ref[...] = hs[:, tlen - 1:tlen, :]


def lru_core(xg, conv_buf, h0, conv_w, conv_b, w_rg, b_rg, w_ig, b_ig, lam):
    bsz, seqlen, w2 = xg.shape
    width = w2 // 2
    bb, tlen = _seq_blocks(bsz, seqlen)
    nj = seqlen // tlen
    row = lambda v: v.reshape(1, 1, width)
    full3 = pl.BlockSpec((1, 1, width), lambda i, j: (0, 0, 0))
    wspec = pl.BlockSpec(w_rg.shape, lambda i, j: (0, 0, 0))
    y, nbuf, hfin = pl.pallas_call(
        functools.partial(_lru_kernel, tlen=tlen),
        grid=(bsz // bb, nj),
        in_specs=[pl.BlockSpec((bb, tlen, width), lambda i, j: (i, j, 0)),
                  pl.BlockSpec((bb, tlen, width), lambda i, j: (i, j, 1)),
                  pl.BlockSpec((bb, CONV_W - 1, width), lambda i, j: (i, 0, 0)),
                  pl.BlockSpec((bb, 1, width), lambda i, j: (i, 0, 0)),
                  pl.BlockSpec((CONV_W, width), lambda i, j: (0, 0)),
                  full3, wspec, full3, wspec, full3, full3],
        out_specs=[pl.BlockSpec((bb * tlen, width), lambda i, j: (i * nj + j, 0)),
                   pl.BlockSpec((bb, CONV_W - 1, width), lambda i, j: (i, 0, 0)),
                   pl.BlockSpec((bb, 1, width), lambda i, j: (i, 0, 0))],
        out_shape=[jax.ShapeDtypeStruct((bsz * seqlen, width), BF16),
                   jax.ShapeDtypeStruct((bsz, CONV_W - 1, width), F32),
                   jax.ShapeDtypeStruct((bsz, 1, width), F32)],
        scratch_shapes=[pltpu.VMEM((bb, tlen + SUBLANES, width), F32), pltpu.VMEM((bb, 1, width), F32)],
        compiler_params=_cparams("parallel", "arbitrary"),
        name="lru_core",
    )(xg, xg, conv_buf, h0.reshape(bsz, 1, width), conv_w, row(conv_b), w_rg, row(b_rg), w_ig, row(b_ig), row(lam))
    return y, nbuf, hfin.reshape(bsz, width)


SSD_HEADDIM = 64
SSD_GROUPS = 8
SSD_HPG = 8
SSD_HEADS = SSD_GROUPS * SSD_HPG
SSD_STATE = 128
SSD_D_INNER = SSD_HEADS * SSD_HEADDIM
SSD_GROUP_W = SSD_HPG * SSD_HEADDIM
SSD_CONV_DIM = SSD_D_INNER + 2 * SSD_GROUPS * SSD_STATE
SSD_QP = 128
SSD_GROUPS_PER_STEP = 4
SSD_GROUPS_PER_STEP_SHORT = 8


def _cumsum_time(x):
    n = x.shape[0]
    d = 1
    while d < n:
        x = x + _shift_time(x, d, 0.0, 0)
        d *= 2
    return x


def _pad_rows(x, rows):
    if x.shape[0] == rows:
        return x
    return jnp.concatenate([x, jnp.zeros((rows - x.shape[0],) + x.shape[1:], x.dtype)], axis=0)


def _lane_half(shape):
    return lax.broadcasted_iota(jnp.int32, shape, 1) < SSD_HEADDIM


def _ssd_kernel(z_ref, x_ref, b_ref, c_ref, dt_ref, bufx_ref, bufb_ref, bufc_ref, s0_ref,
                cwx_ref, cwb_ref, cwc_ref, cbx_ref, cbb_ref, cbc_ref, dtb_ref, alog_ref, dsk_ref, ng_ref,
                y_ref, nbx_ref, nbb_ref, nbc_ref, s_ref, xpx_ref, xpb_ref, xpc_ref, *, q):
    j = pl.program_id(2)
    qp = max(q, SUBLANES)

    @pl.when(j == 0)
    def _():
        s_ref[...] = s0_ref[...]

    def conv(xp_ref, x_in, buf, cw, cb, nb_ref):
        @pl.when(j == 0)
        def _():
            xp_ref[:, SUBLANES - (CONV_W - 1):SUBLANES, :] = buf[...]

        @pl.when(j > 0)
        def _():
            xp_ref[:, SUBLANES - (CONV_W - 1):SUBLANES, :] = xp_ref[:, q + SUBLANES - (CONV_W - 1):q + SUBLANES, :]

        xp_ref[:, SUBLANES:SUBLANES + q, :] = x_in[...]

        @pl.when(j == pl.num_programs(2) - 1)
        def _():
            nb_ref[...] = xp_ref[:, q + SUBLANES - (CONV_W - 1):q + SUBLANES, :]

        return _pad_rows(_silu(_conv_silu_free(xp_ref, cw, cb[...], q)[0]), qp)

    ngrp = x_ref.shape[2] // SSD_GROUP_W
    xs = conv(xpx_ref, x_ref, bufx_ref, cwx_ref, cbx_ref, nbx_ref)
    bm = conv(xpb_ref, b_ref, bufb_ref, cwb_ref, cbb_ref, nbb_ref)
    cm = conv(xpc_ref, c_ref, bufc_ref, cwc_ref, cbc_ref, nbc_ref)

    row = lax.broadcasted_iota(jnp.int32, (qp, LANES), 0)
    causal = lax.broadcasted_iota(jnp.int32, (qp, qp), 0) >= lax.broadcasted_iota(jnp.int32, (qp, qp), 1)
    lane_lo = _lane_half((qp, LANES))
    row_lo = lax.broadcasted_iota(jnp.int32, (LANES, SSD_STATE), 0) < SSD_HEADDIM
    npairs = SSD_HPG // 2
    lanes = lambda t, i: t[:, i * LANES:(i + 1) * LANES]

    def col(v, h, n=LANES):
        return jnp.broadcast_to(v[:, h:h + 1], (qp, n))

    bm3 = jnp.stack([lanes(bm, g) for g in range(ngrp)])
    cm3 = jnp.stack([lanes(cm, g) for g in range(ngrp)])
    cb3 = _dot_nt(cm3, bm3)
    m_head, x_pair, ecum_p, toend_p, el_p = [], [], [], [], []
    for g in range(ngrp):
        dt = jnp.where(row < q, _softplus(_pad_rows(lanes(dt_ref[0], g), qp) + dtb_ref[g]), 0.0)
        cum = _cumsum_time(dt * (-jnp.exp(alog_ref[g])))
        last = cum[qp - 1:qp, :]
        ecum, toend, elast = jnp.exp(cum), jnp.exp(last - cum) * dt, jnp.exp(last)
        cum_t = _transpose_tile(cum)
        dt_t = _transpose_tile(dt)
        for h in range(SSD_HPG):
            seg = col(cum, h, qp) - cum_t[h:h + 1, :]
            m_head.append(cb3[g] * jnp.exp(jnp.where(causal, seg, -jnp.inf)) * dt_t[h:h + 1, :])
        for p in range(npairs):
            x_pair.append(lanes(xs, g * npairs + p))
            ecum_p.append(jnp.where(lane_lo, col(ecum, 2 * p), col(ecum, 2 * p + 1)))
            toend_p.append(jnp.where(lane_lo, col(toend, 2 * p), col(toend, 2 * p + 1)))
            el_p.append(jnp.where(row_lo, jnp.broadcast_to(elast[:, 2 * p:2 * p + 1], (LANES, SSD_STATE)),
                                  jnp.broadcast_to(elast[:, 2 * p + 1:2 * p + 2], (LANES, SSD_STATE))))
    nh, npr = ngrp * SSD_HPG, ngrp * npairs
    x_pair = jnp.stack(x_pair)
    intra = _dot(jnp.stack(m_head), jnp.stack([x_pair[h // 2] for h in range(nh)]))
    s_pair = s_ref[0].reshape(npr, 2 * SSD_HEADDIM, SSD_STATE)
    cm_b = jnp.stack([cm3[i // npairs] for i in range(npr)])
    bm_b = jnp.stack([bm3[i // npairs] for i in range(npr)])
    y = jnp.stack([jnp.where(lane_lo, intra[2 * i], intra[2 * i + 1]) for i in range(npr)])
    y = y + _dot_nt(cm_b, s_pair) * jnp.stack(ecum_p)
    s_new = s_pair * jnp.stack(el_p) + _dot_tn(x_pair * jnp.stack(toend_p), bm_b)
    s_ref[0] = s_new.reshape(nh, SSD_HEADDIM, SSD_STATE)
    y = jnp.concatenate([y[i] for i in range(npr)], axis=-1) + dsk_ref[...] * xs

    yz = y[:q] * _silu(z_ref[0])
    for g in range(ngrp):
        gsl = slice(g * SSD_GROUP_W, (g + 1) * SSD_GROUP_W)
        yg = yz[:, gsl]
        yg = yg * lax.rsqrt(jnp.mean(yg * yg, axis=-1, keepdims=True) + NORM_EPS) * ng_ref[:, gsl]
        y_ref[:, gsl] = yg.astype(y_ref.dtype)


def ssd_core(zxbc, dt_raw, conv_buf, s0, conv_w, conv_b, dt_bias, a_log, d_skip, norm_g):
    bsz, seqlen, _ = zxbc.shape
    q = min(seqlen, SSD_QP)
    nc = seqlen // q
    g8 = SSD_GROUPS

    def grp_pad(v):
        return jnp.pad(v.reshape(g8, 1, SSD_HPG), ((0, 0), (0, 0), (0, LANES - SSD_HPG)))

    d_exp = jnp.repeat(d_skip, SSD_HEADDIM).reshape(1, SSD_D_INNER)
    cb2 = conv_b.reshape(1, SSD_CONV_DIM)
    ngrp = SSD_GROUPS_PER_STEP if nc > 1 else SSD_GROUPS_PER_STEP_SHORT
    xw, sw = ngrp * SSD_GROUP_W, ngrp * SSD_STATE
    xoff = SSD_D_INNER // xw
    boff = 2 * SSD_D_INNER // sw
    coff = boff + SSD_GROUPS // ngrp
    cboff = SSD_D_INNER // sw
    ccoff = cboff + SSD_GROUPS // ngrp
    seq3 = lambda w, off: pl.BlockSpec((1, q, w), lambda b, g, j: (b, j, g + off))
    buf3 = lambda w, off: pl.BlockSpec((1, CONV_W - 1, w), lambda b, g, j: (b, 0, g + off))
    par2 = lambda r, w, off: pl.BlockSpec((r, w), lambda b, g, j: (0, g + off))
    grp = pl.BlockSpec((ngrp, 1, LANES), lambda b, g, j: (g, 0, 0))
    sspec = pl.BlockSpec((1, ngrp * SSD_HPG, SSD_HEADDIM, SSD_STATE), lambda b, g, j: (b, g, 0, 0))
    y, nbx, nbb, nbc, sfin = pl.pallas_call(
        functools.partial(_ssd_kernel, q=q),
        grid=(bsz, g8 // ngrp, nc),
        in_specs=[seq3(xw, 0), seq3(xw, xoff), seq3(sw, boff), seq3(sw, coff),
                  seq3(sw, 0),
                  buf3(xw, 0), buf3(sw, cboff), buf3(sw, ccoff), sspec,
                  par2(CONV_W, xw, 0), par2(CONV_W, sw, cboff), par2(CONV_W, sw, ccoff),
                  par2(1, xw, 0), par2(1, sw, cboff), par2(1, sw, ccoff),
                  grp, grp, par2(1, xw, 0), par2(1, xw, 0)],
        out_specs=[pl.BlockSpec((q, xw), lambda b, g, j: (b * nc + j, g)),
                   buf3(xw, 0), buf3(sw, 0), buf3(sw, 0), sspec],
        out_shape=[jax.ShapeDtypeStruct((bsz * seqlen, SSD_D_INNER), F32),
                   jax.ShapeDtypeStruct((bsz, CONV_W - 1, SSD_D_INNER), F32),
                   jax.ShapeDtypeStruct((bsz, CONV_W - 1, SSD_GROUPS * SSD_STATE), F32),
                   jax.ShapeDtypeStruct((bsz, CONV_W - 1, SSD_GROUPS * SSD_STATE), F32),
                   jax.ShapeDtypeStruct(s0.shape, F32)],
        scratch_shapes=[pltpu.VMEM((1, q + SUBLANES, xw), F32),
                        pltpu.VMEM((1, q + SUBLANES, sw), F32),
                        pltpu.VMEM((1, q + SUBLANES, sw), F32)],
        compiler_params=_cparams("parallel", "parallel", "arbitrary"),
        name="ssd_core",
    )(zxbc, zxbc, zxbc, zxbc, dt_raw, conv_buf, conv_buf, conv_buf, s0,
      conv_w, conv_w, conv_w, cb2, cb2, cb2, grp_pad(dt_bias), grp_pad(a_log), d_exp, norm_g.reshape(1, SSD_D_INNER))
    return y, jnp.concatenate([nbx, nbb, nbc], axis=-1), sfin


def ssd_dt_weight(w_in):
    w_dt = w_in[:, SSD_D_INNER + SSD_CONV_DIM:]
    w_dt = w_dt.reshape(-1, SSD_GROUPS, SSD_HPG)
    return jnp.pad(w_dt, ((0, 0), (0, 0), (0, LANES - SSD_HPG))).reshape(-1, SSD_GROUPS * LANES)


RWKV_HEAD = 64
RWKV_HEADS = D_MODEL // RWKV_HEAD
RWKV_GN_EPS = 64e-5
RWKV_CHUNK = 64
RWKV_PAIRS_PER_STEP = 16
RWKV_PAIRS_PER_STEP_SHORT = 16


def _rwkv_mix_kernel(x_ref, buf_ref, mu_ref, *rest):
    outs, last_ref, carry_ref = rest[:6], rest[6], rest[7]
    j = pl.program_id(1)
    bb, bl, d = x_ref.shape

    @pl.when(j == 0)
    def _():
        carry_ref[...] = buf_ref[...]

    x = x_ref[...]
    t = lax.broadcasted_iota(jnp.int32, x.shape, 1)
    prev = jnp.where(t >= 1, pltpu.roll(x, 1, 1), carry_ref[...])
    carry_ref[...] = x[:, bl - 1:bl, :]
    diff = prev - x
    for s in range(6):
        outs[s][...] = (x + diff * mu_ref[s:s + 1, :]).reshape(bb * bl, d).astype(BF16)

    @pl.when(j == pl.num_programs(1) - 1)
    def _():
        last_ref[...] = x[:, bl - 1:bl, :]


def rwkv_mix(x, shift_buf, mu):
    bsz, seqlen, d = x.shape
    bb, bl = _seq_blocks(bsz, seqlen)
    nj = seqlen // bl
    ospec = pl.BlockSpec((bb * bl, d), lambda i, j: (i * nj + j, 0))
    cspec = pl.BlockSpec((bb, 1, d), lambda i, j: (i, 0, 0))
    outs = pl.pallas_call(
        _rwkv_mix_kernel,
        grid=(bsz // bb, nj),
        in_specs=[pl.BlockSpec((bb, bl, d), lambda i, j: (i, j, 0)), cspec,
                  pl.BlockSpec((6, d), lambda i, j: (0, 0))],
        out_specs=[ospec] * 6 + [cspec],
        out_shape=[jax.ShapeDtypeStruct((bsz * seqlen, d), BF16)] * 6 + [jax.ShapeDtypeStruct((bsz, 1, d), F32)],
        scratch_shapes=[pltpu.VMEM((bb, 1, d), F32)],
        compiler_params=_cparams("parallel", "arbitrary"),
        name="rwkv_mix",
    )(x, shift_buf.reshape(bsz, 1, d), mu)
    return outs[:6], outs[6].reshape(bsz, d)


def _lora_kernel(x_ref, w1_ref, w2_ref, *rest, act, has_bias):
    if has_bias:
        b_ref, o_ref = rest
    else:
        (o_ref,) = rest
    hmid = _dot(x_ref[...], w1_ref[...])
    if act == "tanh":
        hmid = jnp.tanh(hmid)
    elif act == "sigmoid":
        hmid = _sigmoid(hmid)
    out = _dot(hmid, w2_ref[...])
    if has_bias:
        out = out + b_ref[...]
    o_ref[...] = out


def lora(x, w1, w2, bias, act):
    m, k = x.shape
    r = w1.shape[1]
    n = w2.shape[1]
    tm = _pick_tile(m, 512)
    in_specs = [pl.BlockSpec((tm, k), lambda i: (i, 0)), pl.BlockSpec((k, r), lambda i: (0, 0)),
                pl.BlockSpec((r, n), lambda i: (0, 0))]
    args = [x, w1, w2]
    if bias is not None:
        in_specs.append(pl.BlockSpec((1, n), lambda i: (0, 0)))
        args.append(bias.reshape(1, n))
    return pl.pallas_call(
        functools.partial(_lora_kernel, act=act, has_bias=bias is not None),
        grid=(m // tm,),
        in_specs=in_specs,
        out_specs=pl.BlockSpec((tm, n), lambda i: (i, 0)),
        out_shape=jax.ShapeDtypeStruct((m, n), F32),
        compiler_params=_cparams("parallel"),
        name="lora",
    )(*args)


def _half_sum(x, lo):
    s0 = jnp.sum(jnp.where(lo, x, 0.0), axis=-1, keepdims=True)
    s1 = jnp.sum(jnp.where(lo, 0.0, x), axis=-1, keepdims=True)
    return jnp.where(lo, s0, s1)


def _stack_heads(x, lo):
    return jnp.concatenate([jnp.where(lo, x, 0.0), jnp.where(lo, 0.0, x)], axis=0)


def _rwkv_kernel(r_ref, k_ref, v_ref, w_ref, a_ref, g_ref, s0_ref, kk_ref, ka_ref, rk_ref, lg_ref, lb_ref,
                 y_ref, sfin_ref, st_ref, *, c):
    j = pl.program_id(2)
    hd = RWKV_HEAD
    npair = r_ref.shape[2] // LANES
    lo = lax.broadcasted_iota(jnp.int32, (c, LANES), 1) < hd
    zero = jnp.zeros((hd, hd), F32)
    pl_ = lambda p: slice(p * LANES, (p + 1) * LANES)

    @pl.when(j == 0)
    def _():
        for p in range(npair):
            st_ref[p] = jnp.concatenate([jnp.concatenate([s0_ref[0, 2 * p], zero], axis=1),
                                         jnp.concatenate([zero, s0_ref[0, 2 * p + 1]], axis=1)], axis=0)

    def pairwise(fn, *xs):
        return jnp.concatenate([fn(*[x[:, pl_(p)] for x in xs]) for p in range(npair)], axis=-1)

    half_sum = lambda x: pairwise(lambda t: _half_sum(t, lo), x)
    r, k, v = r_ref[0], k_ref[0], v_ref[0]
    lw = -jnp.exp(-_softplus(-w_ref[0]) - 0.5)
    a = _sigmoid(a_ref[0])
    kkr = k * kk_ref[...]
    kk = kkr * lax.rsqrt(half_sum(kkr * kkr) + L2_EPS)
    k = k * (1.0 + (a - 1.0) * ka_ref[...])
    cum = _cumsum_time(lw)
    gam = jnp.exp(cum)
    inv = jnp.exp(-cum)
    stack = lambda x: jnp.stack([_stack_heads(x[:, pl_(p)], lo) for p in range(npair)])
    glast = jnp.stack([gam[c - 1:c, pl_(p)] for p in range(npair)])
    ka_s = stack(jnp.exp(cum - lw) * kk)
    al_s = stack(kk * a * inv)
    k_s = stack(k * inv)
    r_s = stack(r * gam)
    v_s = stack(v)

    n = 2 * c
    ti = lax.broadcasted_iota(jnp.int32, (1, n, n), 1) % c
    si = lax.broadcasted_iota(jnp.int32, (1, n, n), 2) % c
    strict, incl = ti > si, ti >= si
    st = st_ref[...]
    a_mat = jnp.where(strict, _dot_nt(ka_s, al_s), 0.0)
    b_mat = jnp.where(strict, _dot_nt(ka_s, k_s), 0.0)
    d = _dot(_unit_lower_inverse(a_mat, c), _dot_nt(ka_s, st) + _dot(b_mat, v_s))
    o = (_dot_nt(r_s, st) - _dot(jnp.where(incl, _dot_nt(r_s, al_s), 0.0), d)
         + _dot(jnp.where(incl, _dot_nt(r_s, k_s), 0.0), v_s))
    st_ref[...] = st * glast + _dot_tn(v_s, k_s * glast) - _dot_tn(d, al_s * glast)

    o = o[:, :c] + o[:, c:]
    o = jnp.concatenate([o[p] for p in range(npair)], axis=-1)
    mean = half_sum(o) * (1.0 / hd)
    var = half_sum(jnp.square(o - mean)) * (1.0 / hd)
    o = (o - mean) * lax.rsqrt(var + RWKV_GN_EPS) * lg_ref[...] + lb_ref[...]
    o = o + half_sum(r * k * rk_ref[...]) * v
    y_ref[...] = (o * g_ref[0]).astype(y_ref.dtype)

    @pl.when(j == pl.num_programs(2) - 1)
    def _():
        for p in range(npair):
            fin = st_ref[p]
            sfin_ref[0, 2 * p] = fin[:hd, :hd]
            sfin_ref[0, 2 * p + 1] = fin[hd:, hd:]


def rwkv_core(r, k, v, w_raw, a_pre, g, s0, k_k, k_a, r_k, lnx_g, lnx_b):
    bsz, seqlen, d = r.shape
    c = min(seqlen, RWKV_CHUNK)
    nc = seqlen // c
    npair = RWKV_PAIRS_PER_STEP if nc > 1 else RWKV_PAIRS_PER_STEP_SHORT
    wid = npair * LANES
    seq = pl.BlockSpec((1, c, wid), lambda b, p, j: (b, j, p))
    par = pl.BlockSpec((1, wid), lambda b, p, j: (0, p))
    sspec = pl.BlockSpec((1, 2 * npair, RWKV_HEAD, RWKV_HEAD), lambda b, p, j: (b, p, 0, 0))
    row = lambda t: t.reshape(1, d)
    return pl.pallas_call(
        functools.partial(_rwkv_kernel, c=c),
        grid=(bsz, d // wid, nc),
        in_specs=[seq] * 6 + [sspec] + [par] * 5,
        out_specs=[pl.BlockSpec((c, wid), lambda b, p, j: (b * nc + j, p)), sspec],
        out_shape=[jax.ShapeDtypeStruct((bsz * seqlen, d), F32), jax.ShapeDtypeStruct(s0.shape, F32)],
        scratch_shapes=[pltpu.VMEM((npair, LANES, LANES), F32)],
        compiler_params=_cparams("parallel", "parallel", "arbitrary"),
        name="rwkv_core",
    )(r, k, v, w_raw, a_pre, g, s0, row(k_k), row(k_a), row(r_k), row(lnx_g), row(lnx_b))


def rwkv_mixer(hn, shift_buf, s0, mu, w_rkv, w0, w1, w2, a0, a1, a2, g1, g2, k_k, k_a, r_k, lnx_g, lnx_b, w_o):
    bsz, seqlen, d = hn.shape
    (xr, xk, xv, xw, xa, xg), new_shift = rwkv_mix(hn, shift_buf, mu)
    as3 = lambda t: t.reshape(bsz, seqlen, d)
    r, k, v = (as3(matmul(x, w_rkv, w_index=(s,))) for s, x in enumerate((xr, xk, xv)))
    w_raw = as3(lora(xw, w1, w2, w0, "tanh"))
    a_pre = as3(lora(xa, a1, a2, a0, "none"))
    g = as3(lora(xg, g1, g2, None, "sigmoid"))
    y, s_fin = rwkv_core(r, k, v, w_raw, a_pre, g, s0, k_k, k_a, r_k.reshape(-1), lnx_g, lnx_b)
    return matmul(y, w_o), new_shift, s_fin


GDN_K_HEADS = 16
GDN_V_HEADS = 32
GDN_REP = GDN_V_HEADS // GDN_K_HEADS
GDN_HEAD = 128
GDN_KEY_DIM = GDN_K_HEADS * GDN_HEAD
GDN_VAL_DIM = GDN_V_HEADS * GDN_HEAD
GDN_CONV_DIM = 2 * GDN_KEY_DIM + GDN_VAL_DIM
GDN_CP = 64
GDN_HEADS_PER_STEP = 8
GDN_HEADS_PER_STEP_SHORT = 16


def _transpose_tile(x):
    r = x.shape[0]
    return _pad_rows(x, LANES).T[:, :r]


def _l2norm(x):
    return x * lax.rsqrt(jnp.sum(x * x, axis=-1, keepdims=True) + L2_EPS)


def _unit_lower_inverse(a_strict, order=None):
    n = a_strict.shape[-1]
    order = n if order is None else order
    eye = (lax.broadcasted_iota(jnp.int32, (n, n), 0) == lax.broadcasted_iota(jnp.int32, (n, n), 1)).astype(F32)
    p = -a_strict
    t = eye + p
    d = 2
    while d < order:
        p = _dot3(p, p)
        t = t + _dot3(t, p)
        d *= 2
    return t


def _gdn_kernel(q_ref, k_ref, v_ref, z_ref, ba_ref, bufq_ref, bufk_ref, bufv_ref, s0_ref,
                cwq_ref, cwk_ref, cwv_ref, alog_ref, dtb_ref, ng_ref,
                o_ref, nbq_ref, nbk_ref, nbv_ref, s_ref, xpq_ref, xpk_ref, xpv_ref, *, c):
    j = pl.program_id(2)
    cp = max(c, SUBLANES) if c < GDN_CP else GDN_CP
    cp = 1 << (cp - 1).bit_length()

    @pl.when(j == 0)
    def _():
        s_ref[...] = s0_ref[...]

    def conv(xp_ref, x_in, buf, cw, nb_ref):
        @pl.when(j == 0)
        def _():
            xp_ref[:, SUBLANES - (CONV_W - 1):SUBLANES, :] = buf[...]

        @pl.when(j > 0)
        def _():
            xp_ref[:, SUBLANES - (CONV_W - 1):SUBLANES, :] = xp_ref[:, c + SUBLANES - (CONV_W - 1):c + SUBLANES, :]

        xp_ref[:, SUBLANES:SUBLANES + c, :] = x_in[...]

        @pl.when(j == pl.num_programs(2) - 1)
        def _():
            nb_ref[...] = xp_ref[:, c + SUBLANES - (CONV_W - 1):c + SUBLANES, :]

        return _silu(_conv_silu_free(xp_ref, cw, None, c)[0])

    nh = q_ref.shape[2] // GDN_HEAD
    q_all = conv(xpq_ref, q_ref, bufq_ref, cwq_ref, nbq_ref)
    k_all = conv(xpk_ref, k_ref, bufk_ref, cwk_ref, nbk_ref)
    v_all = conv(xpv_ref, v_ref, bufv_ref, cwv_ref, nbv_ref)
    nv = nh * GDN_REP
    live = lax.broadcasted_iota(jnp.int32, (cp, LANES), 0) < c
    ti = lax.broadcasted_iota(jnp.int32, (1, cp, cp), 1)
    si = lax.broadcasted_iota(jnp.int32, (1, cp, cp), 2)
    sl = lambda h: slice(h * GDN_HEAD, (h + 1) * GDN_HEAD)
    qs, ks, vs, gcols, bcols, grows = [], [], [], [], [], []
    for hh in range(nh):
        q = _pad_rows(_l2norm(q_all[:, sl(hh)]) * (GDN_HEAD ** -0.5), cp)
        k = _pad_rows(_l2norm(k_all[:, sl(hh)]), cp)
        ba = _pad_rows(ba_ref[0, :, sl(hh)], cp)
        beta = jnp.where(live, _sigmoid(ba), 0.0)
        g = jnp.where(live, -jnp.exp(alog_ref[hh]) * _softplus(ba + dtb_ref[hh]), 0.0)
        gc = _cumsum_time(g)
        gc_t = _transpose_tile(gc)
        for i in range(GDN_REP):
            qs.append(q)
            ks.append(k)
            vs.append(_pad_rows(v_all[:, sl(hh * GDN_REP + i)], cp))
            gcols.append(gc[:, GDN_REP + i:GDN_REP + i + 1])
            bcols.append(beta[:, i:i + 1])
            grows.append(gc_t[GDN_REP + i:GDN_REP + i + 1, :])
    q, k, v = jnp.stack(qs), jnp.stack(ks), jnp.stack(vs)
    gcol, bcol, grow = jnp.stack(gcols), jnp.stack(bcols), jnp.stack(grows)
    glast = gcol[:, cp - 1:cp, :]
    dec_incl = jnp.exp(jnp.where(ti >= si, gcol - grow, -jnp.inf))
    a_mat = bcol * _dot_nt(k, k) * jnp.where(ti > si, dec_incl, 0.0)
    rhs = jnp.concatenate([v * bcol, k * (bcol * jnp.exp(gcol))], axis=-1)
    sol = _dot(_unit_lower_inverse(a_mat), rhs)
    u, w = sol[:, :, :GDN_HEAD], sol[:, :, GDN_HEAD:]
    s = s_ref[0]
    v_new = u - _dot(w, s)
    o = _dot(q * jnp.exp(gcol), s) + _dot(_dot_nt(q, k) * dec_incl, v_new)
    s_ref[0] = s * jnp.exp(glast) + _dot_tn(k * jnp.exp(glast - gcol), v_new)
    o = o[:, :c]
    o = o * lax.rsqrt(jnp.mean(o * o, axis=-1, keepdims=True) + NORM_EPS) * ng_ref[...]
    for vh in range(nv):
        o_ref[:, sl(vh)] = (o[vh] * _silu(z_ref[0, :, sl(vh)])).astype(o_ref.dtype)


def gdn_core(qkvz, ba, conv_buf, s0, conv_w, a_log, dt_bias, norm_g):
    bsz, seqlen, _ = qkvz.shape
    c = min(seqlen, GDN_CP)
    nc = seqlen // c
    kh = GDN_K_HEADS
    nh = GDN_HEADS_PER_STEP if nc > 1 else GDN_HEADS_PER_STEP_SHORT
    hd = nh * GDN_HEAD
    vw = GDN_REP * hd

    def head_pad(v):
        return jnp.pad(v.reshape(kh, 1, GDN_REP), ((0, 0), (0, 0), (GDN_REP, LANES - 2 * GDN_REP)))

    koff = GDN_KEY_DIM // hd
    voff = 2 * GDN_KEY_DIM // vw
    zoff = GDN_CONV_DIM // vw
    seq3 = lambda w, off: pl.BlockSpec((1, c, w), lambda b, h, j: (b, j, h + off))
    buf3 = lambda w, off: pl.BlockSpec((1, CONV_W - 1, w), lambda b, h, j: (b, 0, h + off))
    par2 = lambda w, off: pl.BlockSpec((CONV_W, w), lambda b, h, j: (0, h + off))
    hp = pl.BlockSpec((nh, 1, LANES), lambda b, h, j: (h, 0, 0))
    sspec = pl.BlockSpec((1, nh * GDN_REP, GDN_HEAD, GDN_HEAD), lambda b, h, j: (b, h, 0, 0))
    o, nbq, nbk, nbv, sfin = pl.pallas_call(
        functools.partial(_gdn_kernel, c=c),
        grid=(bsz, kh // nh, nc),
        in_specs=[seq3(hd, 0), seq3(hd, koff), seq3(vw, voff), seq3(vw, zoff), seq3(hd, 0),
                  buf3(hd, 0), buf3(hd, koff), buf3(vw, voff), sspec,
                  par2(hd, 0), par2(hd, koff), par2(vw, voff), hp, hp,
                  pl.BlockSpec((1, GDN_HEAD), lambda b, h, j: (0, 0))],
        out_specs=[pl.BlockSpec((c, vw), lambda b, h, j: (b * nc + j, h)),
                   buf3(hd, 0), buf3(hd, 0), buf3(vw, 0), sspec],
        out_shape=[jax.ShapeDtypeStruct((bsz * seqlen, GDN_VAL_DIM), F32),
                   jax.ShapeDtypeStruct((bsz, CONV_W - 1, GDN_KEY_DIM), F32),
                   jax.ShapeDtypeStruct((bsz, CONV_W - 1, GDN_KEY_DIM), F32),
                   jax.ShapeDtypeStruct((bsz, CONV_W - 1, GDN_VAL_DIM), F32),
                   jax.ShapeDtypeStruct(s0.shape, F32)],
        scratch_shapes=[pltpu.VMEM((1, c + SUBLANES, hd), F32), pltpu.VMEM((1, c + SUBLANES, hd), F32),
                        pltpu.VMEM((1, c + SUBLANES, vw), F32)],
        compiler_params=_cparams("parallel", "parallel", "arbitrary"),
        name="gdn_core",
    )(qkvz, qkvz, qkvz, qkvz, ba, conv_buf, conv_buf, conv_buf, s0,
      conv_w, conv_w, conv_w, head_pad(a_log), head_pad(dt_bias), norm_g.reshape(1, GDN_HEAD))
    return o, jnp.concatenate([nbq, nbk, nbv], axis=-1), sfin


def gdn_ba_weight(w_in):
    tail = w_in[:, GDN_CONV_DIM + GDN_VAL_DIM:]
    b_w = tail[:, :GDN_V_HEADS].reshape(-1, GDN_K_HEADS, GDN_REP)
    a_w = tail[:, GDN_V_HEADS:].reshape(-1, GDN_K_HEADS, GDN_REP)
    blk = jnp.concatenate([b_w, a_w], axis=-1)
    return jnp.pad(blk, ((0, 0), (0, 0), (0, LANES - 2 * GDN_REP))).reshape(-1, GDN_K_HEADS * LANES)


N_EXPERTS = 32
TOP_K = 4
D_FF = 2048
SWIGLU_LIMIT = 7.0
SWIGLU_ALPHA = 1.702
MOE_ROWS = 512
MOE_ROUTER_TOKENS = 512
MOE_TN_UP = 1024
MOE_TN_DOWN = 2048
MOE_TOKENS_PER_STEP = 512


def _router_kernel(x_ref, w_ref, b_ref, e_ref, g_ref, r_ref, cnt_ref, xp_ref, carry_ref):
    @pl.when(pl.program_id(0) == 0)
    def _():
        carry_ref[...] = jnp.zeros(carry_ref.shape, F32)

    xp_ref[...] = _pack_rows(x_ref[...])
    logits = _dot6(x_ref[...], w_ref[...]) + b_ref[...]
    tm = logits.shape[0]
    lane = lax.broadcasted_iota(jnp.int32, logits.shape, 1)
    logits = jnp.where(lane < N_EXPERTS, logits, -jnp.inf)
    e_out = jnp.zeros(logits.shape, jnp.int32)
    g_out = jnp.zeros(logits.shape, F32)
    picked = jnp.zeros(logits.shape, F32)
    top, idxs = None, []
    for kth in range(TOP_K):
        m = jnp.max(logits, axis=-1, keepdims=True)
        idx = jnp.min(jnp.where(logits == m, lane, LANES), axis=-1, keepdims=True)
        top = m if top is None else top
        idxs.append(idx)
        e_out = jnp.where(lane == kth, idx, e_out)
        g_out = jnp.where(lane == kth, jnp.exp(m - top), g_out)
        picked = jnp.where(lane == idx, 1.0, picked)
        logits = jnp.where(lane == idx, -jnp.inf, logits)
    e_ref[0] = e_out.T[:SUBLANES]
    g_ref[...] = g_out / jnp.sum(g_out, axis=-1, keepdims=True)
    earlier = (lax.broadcasted_iota(jnp.int32, (tm, tm), 0) > lax.broadcasted_iota(jnp.int32, (tm, tm), 1))
    before = carry_ref[...] + _dot(earlier.astype(F32), picked)
    r_out = jnp.zeros(logits.shape, jnp.int32)
    for kth in range(TOP_K):
        rk = jnp.sum(jnp.where(lane == idxs[kth], before, 0.0), axis=-1, keepdims=True)
        r_out = jnp.where(lane == kth, rk.astype(jnp.int32), r_out)
    r_ref[0] = r_out.T[:SUBLANES]
    carry_ref[...] = carry_ref[...] + jnp.sum(picked, axis=0, keepdims=True)
    cnt_ref[...] = carry_ref[...].astype(jnp.int32)


def moe_router(x, w_router, b_router, layer):
    n_tok, d = x.shape
    tm = MOE_ROUTER_TOKENS
    assert n_tok % tm == 0
    w = jnp.pad(w_router[layer], ((0, 0), (0, LANES - N_EXPERTS)))
    b = jnp.pad(b_router[layer], (0, LANES - N_EXPERTS)).reshape(1, LANES)
    tok = pl.BlockSpec((tm, LANES), lambda i: (i, 0))
    tile = pl.BlockSpec((1, SUBLANES, tm), lambda i: (i, 0, 0))
    return pl.pallas_call(
        _router_kernel,
        grid=(n_tok // tm,),
        in_specs=[pl.BlockSpec((tm, d), lambda i: (i, 0)), pl.BlockSpec((d, LANES), lambda i: (0, 0)),
                  pl.BlockSpec((1, LANES), lambda i: (0, 0))],
        out_specs=[tile, tok, tile, pl.BlockSpec((1, LANES), lambda i: (0, 0)),
                   pl.BlockSpec((tm, d // 2), lambda i: (i, 0))],
        out_shape=[jax.ShapeDtypeStruct((n_tok // tm, SUBLANES, tm), jnp.int32),
                   jax.ShapeDtypeStruct((n_tok, LANES), F32),
                   jax.ShapeDtypeStruct((n_tok // tm, SUBLANES, tm), jnp.int32),
                   jax.ShapeDtypeStruct((1, LANES), jnp.int32),
                   jax.ShapeDtypeStruct((n_tok, d // 2), jnp.uint32)],
        scratch_shapes=[pltpu.VMEM((1, LANES), F32)],
        compiler_params=_cparams("arbitrary"),
        name="moe_router",
    )(x, w, b)


def moe_blocks(counts, n_pairs):
    nb_e = (counts + MOE_ROWS - 1) // MOE_ROWS
    bend = jnp.cumsum(nb_e)
    return nb_e, bend - nb_e, bend, n_pairs // MOE_ROWS + N_EXPERTS


def _issue_and_drain(n_tok, start_one, wait_one):
    def start(t, carry):
        for kth in range(TOP_K):
            start_one(t, kth)
        return carry

    def wait(t, carry):
        for _ in range(TOP_K):
            wait_one()
        return carry

    lax.fori_loop(0, n_tok, start, 0)
    lax.fori_loop(0, n_tok, wait, 0)


def _pair_tile_spec(e_t, tm):
    per = e_t.shape[2] // tm
    return pl.BlockSpec((1, SUBLANES, tm), lambda i, *_: (i // per, 0, i % per), memory_space=pltpu.SMEM)


def _dispatch_kernel(bstart_ref, nb_ref, cnt_ref, slot_ref, x_ref, xs_ref, zero_ref, sem):
    tm = x_ref.shape[0]

    @pl.when(pl.program_id(0) == 0)
    def _():
        zero_ref[...] = jnp.zeros(zero_ref.shape, zero_ref.dtype)

        def zero_row(row):
            return pltpu.make_async_copy(zero_ref.at[pl.ds(0, 1)], xs_ref.at[pl.ds(row, 1)], sem)

        def zero_piece(row):
            return pltpu.make_async_copy(zero_ref, xs_ref.at[pl.ds(pl.multiple_of(row, SUBLANES), SUBLANES)], sem)

        def drain(n, copy):
            def wait(c, cc):
                copy.wait()
                return cc

            lax.fori_loop(0, n, wait, 0)

        def fill(e, carry):
            first = bstart_ref[e] * MOE_ROWS + cnt_ref[e]
            singles = (-cnt_ref[e]) % SUBLANES
            pieces = (nb_ref[e] * MOE_ROWS - cnt_ref[e] - singles) // SUBLANES

            def start_row(c, cc):
                zero_row(first + c).start()
                return cc

            def start_piece(c, cc):
                zero_piece(first + singles + c * SUBLANES).start()
                return cc

            lax.fori_loop(0, singles, start_row, 0)
            lax.fori_loop(0, pieces, start_piece, 0)
            drain(singles, zero_row(0))
            drain(pieces, zero_piece(0))
            return carry

        lax.fori_loop(0, N_EXPERTS + 1, fill, 0)

    def row_copy(src_row, slot):
        return pltpu.make_async_copy(x_ref.at[pl.ds(src_row, 1)], xs_ref.at[pl.ds(slot, 1)], sem)

    def start_one(t, kth):
        row_copy(t, slot_ref[0, kth, t]).start()

    _issue_and_drain(tm, start_one, lambda: row_copy(0, 0).wait())


def moe_pair_slots(e_t, r_t, bstart):
    first = jnp.zeros(e_t.shape, jnp.int32)
    for e in range(N_EXPERTS):
        first = jnp.where(e_t == e, bstart[e], first)
    return first * MOE_ROWS + r_t


def moe_dispatch(x_rows, slot_t, bstart, nb_e, counts, n_blocks):
    n_tok, width = x_rows.shape
    tm = MOE_TOKENS_PER_STEP
    assert n_tok % tm == 0
    smem = _pair_tile_spec(slot_t, tm)
    used = bstart[-1:] + nb_e[-1:]
    bstart = jnp.concatenate([bstart, used])
    nb_e = jnp.concatenate([nb_e, n_blocks - used])
    counts = jnp.concatenate([counts, jnp.zeros((1,), counts.dtype)])
    return pl.pallas_call(
        _dispatch_kernel,
        grid_spec=pltpu.PrefetchScalarGridSpec(
            num_scalar_prefetch=3, grid=(n_tok // tm,),
            in_specs=[smem, pl.BlockSpec((tm, width), lambda i, *_: (i, 0))],
            out_specs=pl.BlockSpec(memory_space=pl.ANY),
            scratch_shapes=[pltpu.VMEM((SUBLANES, width), x_rows.dtype), pltpu.SemaphoreType.DMA(())]),
        out_shape=jax.ShapeDtypeStruct((n_blocks * MOE_ROWS, width), x_rows.dtype),
        compiler_params=_cparams("arbitrary"),
        name="moe_dispatch",
    )(bstart, nb_e, counts, slot_t, x_rows)


def _combine_kernel(slot_ref, g_ref, ys_ref, o_ref, buf_ref, sem):
    tm = o_ref.shape[0]

    def row_copy(slot, dst_row):
        return pltpu.make_async_copy(ys_ref.at[pl.ds(slot, 1)], buf_ref.at[pl.ds(dst_row, 1)], sem)

    def start_one(t, kth):
        row_copy(slot_ref[0, kth, t], kth * tm + t).start()

    _issue_and_drain(tm, start_one, lambda: row_copy(0, 0).wait())
    acc = None
    for kth in range(TOP_K):
        term = g_ref[:, kth:kth + 1] * buf_ref[kth * tm:(kth + 1) * tm, :]
        acc = term if acc is None else acc + term
    o_ref[...] = acc


def moe_combine(ys, slot_t, gates):
    d = ys.shape[1]
    n_tok = gates.shape[0]
    tm = MOE_TOKENS_PER_STEP
    return pl.pallas_call(
        _combine_kernel,
        grid=(n_tok // tm,),
        in_specs=[_pair_tile_spec(slot_t, tm), pl.BlockSpec((tm, LANES), lambda i: (i, 0)),
                  pl.BlockSpec(memory_space=pl.ANY)],
        out_specs=pl.BlockSpec((tm, d), lambda i: (i, 0)),
        out_shape=jax.ShapeDtypeStruct((n_tok, d), F32),
        scratch_shapes=[pltpu.VMEM((TOP_K * tm, d), F32), pltpu.SemaphoreType.DMA(())],
        compiler_params=_cparams("arbitrary"),
        name="moe_combine",
    )(slot_t, gates, ys)


def moe_items(nb_e, bstart, bend, n_blocks, nt):
    i = jnp.arange(n_blocks * nt, dtype=jnp.int32)
    used = bend[-1]
    e_i = jnp.minimum(jnp.sum((i[:, None] >= (bend * nt)[None, :]).astype(jnp.int32), axis=1), N_EXPERTS - 1)
    local = i - bstart[e_i] * nt
    nbe = jnp.maximum(nb_e[e_i], 1)
    valid = i < used * nt
    spare = i - used * nt
    blk = jnp.where(valid, bstart[e_i] + local % nbe, used + spare // nt)
    blk_in = jnp.where(valid, blk, 0)
    n_out = jnp.where(valid, local // nbe, spare % nt)
    e_last = jnp.max(jnp.where(nb_e > 0, jnp.arange(N_EXPERTS, dtype=jnp.int32), 0))
    e_w = jnp.where(valid, e_i, e_last)
    n_w = jnp.where(valid, local // nbe, nt - 1)
    first = (valid & (local % nbe == 0)).astype(jnp.int32)
    i32 = lambda a: a.astype(jnp.int32)
    return i32(blk_in), i32(blk), i32(e_w), i32(n_w), i32(n_out), first, i32(valid)


def _unpack_rows(words):
    lo = lax.bitcast_convert_type(words << 16, F32)
    hi = lax.bitcast_convert_type(words & jnp.uint32(0xFFFF0000), F32)
    return jnp.concatenate([lo, hi], axis=-1).astype(BF16)


def _pack_rows(x):
    half = x.shape[-1] // 2
    bits = lambda v: lax.bitcast_convert_type(v.astype(BF16).astype(F32), jnp.uint32)
    return (bits(x[:, half:]) & jnp.uint32(0xFFFF0000)) | (bits(x[:, :half]) >> 16)


def _gmm_kernel(bin_ref, blk_ref, ew_ref, nw_ref, no_ref, first_ref, valid_ref, x_ref, *rest, glu):
    i = pl.program_id(0)
    if glu:
        wg_ref, wu_ref, bg_ref, bu_ref, o_ref, wgb_ref, wub_ref = rest
    else:
        w_ref, b_ref, o_ref, wb_ref = rest

    @pl.when(first_ref[i] == 1)
    def _():
        if glu:
            wgb_ref[...] = wg_ref[...].astype(BF16)
            wub_ref[...] = wu_ref[...].astype(BF16)
        else:
            wb_ref[...] = w_ref[...].astype(BF16)

    @pl.when(valid_ref[i] == 1)
    def _():
        if glu:
            x = _unpack_rows(x_ref[...])
            gate = jnp.dot(x, wgb_ref[...], preferred_element_type=F32) + bg_ref[...]
            up = jnp.dot(x, wub_ref[...], preferred_element_type=F32) + bu_ref[...]
            gate = jnp.minimum(gate, SWIGLU_LIMIT)
            up = jnp.clip(up, -SWIGLU_LIMIT, SWIGLU_LIMIT)
            out = (up + 1.0) * (gate * _sigmoid(SWIGLU_ALPHA * gate))
        else:
            out = jnp.dot(x_ref[...], wb_ref[...], preferred_element_type=F32) + b_ref[...]
        o_ref[...] = out.astype(o_ref.dtype)

    @pl.when(valid_ref[i] == 0)
    def _():
        o_ref[...] = jnp.zeros(o_ref.shape, o_ref.dtype)


def grouped_matmul(xs, w, b, layer, items, *, glu):
    k = w.shape[-2]
    n_blocks = xs.shape[0] // MOE_ROWS
    n_out = w.shape[-1] // 2 if glu else w.shape[-1]
    tn = MOE_TN_UP if glu else MOE_TN_DOWN
    n_items = items[0].shape[0]
    b4 = b.reshape(b.shape[0], b.shape[1], 1, b.shape[2])
    wspec = lambda off: pl.BlockSpec((None, None, k, tn),
                                     lambda i, bi, blk, ew, nw, no, fi, va: (layer, ew[i], 0, nw[i] + off))
    bspec = lambda off: pl.BlockSpec((None, None, 1, tn),
                                     lambda i, bi, blk, ew, nw, no, fi, va: (layer, ew[i], 0, nw[i] + off))
    xspec = pl.BlockSpec((MOE_ROWS, xs.shape[1]), lambda i, bi, blk, ew, nw, no, fi, va: (bi[i], 0))
    ospec = pl.BlockSpec((MOE_ROWS, tn), lambda i, bi, blk, ew, nw, no, fi, va: (blk[i], no[i]))
    if glu:
        half = n_out // tn
        in_specs = [xspec, wspec(0), wspec(half), bspec(0), bspec(half)]
        args = (xs, w, w, b4, b4)
        scratch = [pltpu.VMEM((k, tn), BF16)] * 2
    else:
        in_specs = [xspec, wspec(0), bspec(0)]
        args = (xs, w, b4)
        scratch = [pltpu.VMEM((k, tn), BF16)]
    return pl.pallas_call(
        functools.partial(_gmm_kernel, glu=glu),
        grid_spec=pltpu.PrefetchScalarGridSpec(
            num_scalar_prefetch=7, grid=(n_items,), in_specs=in_specs, out_specs=ospec, scratch_shapes=scratch),
        out_shape=jax.ShapeDtypeStruct((n_blocks * MOE_ROWS, n_out), BF16 if glu else F32),
        compiler_params=_cparams("arbitrary"),
        name="moe_up" if glu else "moe_down",
    )(*items, *args)


def moe_ffn(hn, layer, w_router, b_router, w_gu, b_gu, w_down, b_down):
    n_tok, d = hn.shape
    e_t, gates, r_t, counts, x_rows = moe_router(hn, w_router, b_router, layer)
    counts = counts[0, :N_EXPERTS]
    nb_e, bstart, bend, n_blocks = moe_blocks(counts, n_tok * TOP_K)
    slot_t = moe_pair_slots(e_t, r_t, bstart)
    xs = moe_dispatch(x_rows, slot_t, bstart, nb_e, counts, n_blocks)
    hmid = grouped_matmul(xs, w_gu, b_gu, layer, moe_items(nb_e, bstart, bend, n_blocks, D_FF // MOE_TN_UP), glu=True)
    ys = grouped_matmul(hmid, w_down, b_down, layer,
                        moe_items(nb_e, bstart, bend, n_blocks, D_MODEL // MOE_TN_DOWN), glu=False)
    return moe_combine(ys, slot_t, gates)


def ssd_mixer(hn, conv_buf, s0, w_in, conv_w, conv_b, dt_bias, a_log, d_skip, norm_g, w_out):
    bsz, seqlen, d = hn.shape
    h2 = hn.reshape(bsz * seqlen, d)
    zxbc = matmul(h2, w_in, w_index=(0,), n_cols=SSD_D_INNER + SSD_CONV_DIM).reshape(bsz, seqlen, -1)
    dt_raw = matmul(h2, ssd_dt_weight(w_in[0])).reshape(bsz, seqlen, -1)
    y, new_buf, s_fin = ssd_core(zxbc, dt_raw, conv_buf, s0, conv_w[0], conv_b[0], dt_bias[0], a_log[0], d_skip[0],
                                 norm_g[0])
    return matmul(y, w_out, w_index=(0,)), new_buf, s_fin


def lru_mixer(hn, conv_buf, h0, w_in, conv_w, conv_b, w_rg, b_rg, w_ig, b_ig, lam, w_out):
    bsz, seqlen, d = hn.shape
    xg = matmul(hn.reshape(bsz * seqlen, d), w_in, w_index=(0,)).reshape(bsz, seqlen, -1)
    y, new_buf, h_fin = lru_core(xg, conv_buf, h0, conv_w[0], conv_b[0], w_rg[0], b_rg[0].reshape(-1), w_ig[0],
                                 b_ig[0].reshape(-1), lam[0])
    return matmul(y, w_out, w_index=(0,)), new_buf, h_fin


def gdn_mixer(hn, conv_buf, s0, w_in, conv_w, a_log, dt_bias, norm_g, w_out):
    bsz, seqlen, d = hn.shape
    h2 = hn.reshape(bsz * seqlen, d)
    qkvz = matmul(h2, w_in, w_index=(0,), n_cols=GDN_CONV_DIM + GDN_VAL_DIM).reshape(bsz, seqlen, -1)
    ba = matmul(h2, gdn_ba_weight(w_in[0])).reshape(bsz, seqlen, -1)
    o, new_buf, s_fin = gdn_core(qkvz, ba, conv_buf, s0, conv_w[0], a_log[0], dt_bias[0], norm_g[0])
    return matmul(o, w_out, w_index=(0,)), new_buf, s_fin


def kernel(x_prompt, x_sample, c_prompt, c_sample, state_ssd, cache_ssd_conv, state_rwkv, cache_rwkv_shift, state_lru, cache_lru_conv, state_gdn, cache_gdn_conv, w_ada, b_ada, norm_g, final_g, ssd_w_in, ssd_conv_w, ssd_conv_b, ssd_dt_bias, ssd_a_log, ssd_d, ssd_norm_g, ssd_w_out, rwkv_mu, rwkv_w_rkv, rwkv_w0, rwkv_w1, rwkv_w2, rwkv_a0, rwkv_a1, rwkv_a2, rwkv_g1, rwkv_g2, rwkv_k_k, rwkv_k_a, rwkv_r_k, rwkv_lnx_g, rwkv_lnx_b, rwkv_w_o, lru_w_in, lru_conv_w, lru_conv_b, lru_w_rg, lru_b_rg, lru_w_ig, lru_b_ig, lru_lambda, lru_w_out, gdn_w_in, gdn_conv_w, gdn_a_log, gdn_dt_bias, gdn_norm_g, gdn_w_out, moe_w_router, moe_b_router, moe_w_gu, moe_b_gu, moe_w_down, moe_b_down):
    assert DEPTH == 4 and w_ada.shape[0] == DEPTH
    d = D_MODEL
    xs = [x_prompt, x_sample]
    n_seq = [x.shape[0] for x in xs]
    n_rows = [x.shape[0] * x.shape[1] for x in xs]
    row0 = [0, n_rows[0]]

    cond = jnp.concatenate([c_prompt, c_sample], axis=0)
    pad = (-cond.shape[0]) % (2 * SUBLANES)
    cond = jnp.pad(cond, ((0, pad), (0, 0)))
    mods = []
    for i in range(DEPTH):
        m = matmul(cond, w_ada, b_ada, w_index=(i,), pre_silu=True, tn=ADA_TN)
        mods.append([m[:n_seq[0]].reshape(n_seq[0], 1, 6 * d),
                     m[n_seq[0]:n_seq[0] + n_seq[1]].reshape(n_seq[1], 1, 6 * d)])

    def zeros_like_state(s, bsz):
        return jnp.zeros((bsz,) + s.shape[2:], F32)

    st = {
        "ssd": [zeros_like_state(state_ssd, n_seq[0]), state_ssd[0]],
        "ssd_conv": [zeros_like_state(cache_ssd_conv, n_seq[0]), cache_ssd_conv[0]],
        "rwkv": [zeros_like_state(state_rwkv, n_seq[0]), state_rwkv[0]],
        "rwkv_shift": [zeros_like_state(cache_rwkv_shift, n_seq[0]), cache_rwkv_shift[0]],
        "lru": [zeros_like_state(state_lru, n_seq[0]), state_lru[0]],
        "lru_conv": [zeros_like_state(cache_lru_conv, n_seq[0]), cache_lru_conv[0]],
        "gdn": [zeros_like_state(state_gdn, n_seq[0]), state_gdn[0]],
        "gdn_conv": [zeros_like_state(cache_gdn_conv, n_seq[0]), cache_gdn_conv[0]],
    }
    new = {}

    hn = [norm_mod(xs[gi], norm_g[0, 0], mods[0][gi], 0) for gi in range(2)]
    outs = None
    for i in range(DEPTH):
        ys = []
        for gi in range(2):
            if i == 0:
                y, buf, s = ssd_mixer(hn[gi], st["ssd_conv"][gi], st["ssd"][gi], ssd_w_in, ssd_conv_w, ssd_conv_b,
                                      ssd_dt_bias, ssd_a_log, ssd_d, ssd_norm_g, ssd_w_out)
                new.setdefault("ssd", []).append(s)
                new.setdefault("ssd_conv", []).append(buf)
            elif i == 1:
                y, buf, s = rwkv_mixer(hn[gi], st["rwkv_shift"][gi], st["rwkv"][gi], rwkv_mu[0], rwkv_w_rkv[0],
                                       rwkv_w0[0], rwkv_w1[0], rwkv_w2[0], rwkv_a0[0], rwkv_a1[0], rwkv_a2[0],
                                       rwkv_g1[0], rwkv_g2[0], rwkv_k_k[0], rwkv_k_a[0], rwkv_r_k[0],
                                       rwkv_lnx_g[0], rwkv_lnx_b[0], rwkv_w_o[0])
                new.setdefault("rwkv", []).append(s)
                new.setdefault("rwkv_shift", []).append(buf)
            elif i == 2:
                y, buf, s = lru_mixer(hn[gi], st["lru_conv"][gi], st["lru"][gi], lru_w_in, lru_conv_w, lru_conv_b,
                                      lru_w_rg, lru_b_rg, lru_w_ig, lru_b_ig, lru_lambda, lru_w_out)
                new.setdefault("lru", []).append(s)
                new.setdefault("lru_conv", []).append(buf)
            else:
                y, buf, s = gdn_mixer(hn[gi], st["gdn_conv"][gi], st["gdn"][gi], gdn_w_in, gdn_conv_w, gdn_a_log,
                                      gdn_dt_bias, gdn_norm_g, gdn_w_out)
                new.setdefault("gdn", []).append(s)
                new.setdefault("gdn_conv", []).append(buf)
            ys.append(y)
        hn2 = []
        for gi in range(2):
            xs[gi], h2 = resid_norm(xs[gi], ys[gi], 0, norm_g[i, 1], mods[i][gi], 0, mods[i][gi], 1)
            hn2.append(h2.reshape(n_rows[gi], d))
        moe = moe_ffn(jnp.concatenate(hn2, axis=0), i, moe_w_router, moe_b_router, moe_w_gu, moe_b_gu, moe_w_down,
                      moe_b_down)
        if i + 1 < DEPTH:
            for gi in range(2):
                xs[gi], hn[gi] = resid_norm(xs[gi], moe, row0[gi], norm_g[i + 1, 0], mods[i][gi], 1,
                                            mods[i + 1][gi], 0)
        else:
            outs = [resid_final(xs[gi], moe, row0[gi], final_g, mods[i][gi], 1) for gi in range(2)]

    res = [outs[0], outs[1]]
    for key in ("ssd", "ssd_conv", "rwkv", "rwkv_shift", "lru", "lru_conv", "gdn", "gdn_conv"):
        res += [new[key][0][None], new[key][1][None]]
    return tuple(res)
```
